```python
import math
import jax, jax.numpy as jnp
from jax import lax
import numpy as np

D_MODEL = 2048
BATCH = 4
SEQ = 2048
DEPTH = 2
DEC_BATCH = 32
DEC_SEQ = 1
PAST_LEN = 8192
PAGE_SIZE = 128

MIX_WIDTH = D_MODEL
RET_WIDTH = MIX_WIDTH // 2
S5_WIDTH = MIX_WIDTH - RET_WIDTH
RET_HEADS = 4
RET_DK = RET_WIDTH // RET_HEADS
RET_DV = RET_WIDTH // RET_HEADS
RET_CHUNK = 128
ROPE_BASE = 10000.0
S5_GROUP_CH = 16
S5_GROUPS = S5_WIDTH // S5_GROUP_CH
S5_STATE = 64
ATT_HD = 128
ATT_HEADS = D_MODEL // ATT_HD
ATT_KV_HEADS = 4
IDX_HEADS = 16
IDX_DIM = 128
INDEX_TOPK = 256
ATT_Q_BLOCK = 128
D_FF = 5632
FFN_RES = 0.5
N_SUBLAYERS = 3
EPS = 1e-6
PAGE_POOL_NUM = 5
PAGE_POOL_DEN = 4

EVEN_IN = 4 * RET_WIDTH + S5_WIDTH
ODD_IN = ATT_HEADS * ATT_HD + 2 * ATT_KV_HEADS * ATT_HD + IDX_HEADS * IDX_DIM + IDX_DIM + IDX_HEADS

kernel_name = 'hybrid_retention_s5_dsa_decoder_step'


def rmsnorm(x, g):
    x32 = x.astype(jnp.float32)
    y = x32 * lax.rsqrt(jnp.mean(x32 * x32, axis=-1, keepdims=True) + EPS)
    return (y * g.astype(jnp.float32)).astype(x.dtype)


def split_cols(z, widths):
    cuts = [int(v) for v in np.cumsum(widths)[:-1]]
    return jnp.split(z, cuts, axis=-1)


def swiglu(h, w_in, w_out):
    a, b = jnp.split(h @ w_in, 2, axis=-1)
    return (jax.nn.silu(a) * b) @ w_out


def rotary(x, pos):
    half = x.shape[-1] // 2
    freqs = ROPE_BASE ** (-jnp.arange(half, dtype=jnp.float32) / half)
    ang = pos.astype(jnp.float32)[:, None] * freqs[None, :]
    cos = jnp.cos(ang)[None, :, None, :]
    sin = jnp.sin(ang)[None, :, None, :]
    x32 = x.astype(jnp.float32)
    x1, x2 = x32[..., :half], x32[..., half:]
    return jnp.concatenate([x1 * cos - x2 * sin, x1 * sin + x2 * cos], axis=-1).astype(x.dtype)


def retention(q, k, v, state0, chunk):
    B_, L, H, _ = q.shape
    DV = v.shape[-1]
    n = L // chunk
    log_g = jnp.log1p(-(2.0 ** (-5.0 - jnp.arange(H, dtype=jnp.float32))))
    idx = jnp.arange(chunk, dtype=jnp.float32)
    diff = idx[:, None] - idx[None, :]
    dmat = jnp.where(diff[None] >= 0, jnp.exp(jnp.maximum(diff, 0.0)[None] * log_g[:, None, None]), 0.0)
    q_dec = jnp.exp((idx + 1.0)[None, :] * log_g[:, None]).T
    k_dec = jnp.exp((chunk - 1.0 - idx)[None, :] * log_g[:, None])
    c_dec = jnp.exp(chunk * log_g)

    def to_chunks(t):
        return t.astype(jnp.float32).reshape(B_, n, chunk, H, t.shape[-1]).transpose(1, 0, 2, 3, 4)

    def step(S, inp):
        qc, kc, vc = inp
        a = jnp.einsum('bihd,bjhd->bhij', qc, kc) * dmat[None]
        o = jnp.einsum('bhij,bjhe->bihe', a, vc)
        o = o + jnp.einsum('bihd,bhde->bihe', qc, S) * q_dec[None, :, :, None]
        S = S * c_dec[None, :, None, None] + jnp.einsum('bjhd,bjhe,hj->bhde', kc, vc, k_dec)
        return S, o

    S, o = lax.scan(step, state0.astype(jnp.float32), (to_chunks(q), to_chunks(k), to_chunks(v)))
    o = o.transpose(1, 0, 2, 3, 4).reshape(B_, L, H, DV)
    return o, S


def head_norm(o):
    mu = jnp.mean(o, axis=-1, keepdims=True)
    var = jnp.mean(jnp.square(o - mu), axis=-1, keepdims=True)
    return (o - mu) * lax.rsqrt(var + 1e-5)


def _complex_affine_combine(e1, e2):
    a1r, a1i, b1r, b1i = e1
    a2r, a2i, b2r, b2i = e2
    ar = a2r * a1r - a2i * a1i
    ai = a2r * a1i + a2i * a1r
    br = a2r * b1r - a2i * b1i + b2r
    bi = a2r * b1i + a2i * b1r + b2i
    return (ar, ai, br, bi)


def s5_layer(u, x0_re, x0_im, a_re, a_im, log_dt, b_re, b_im, c_re, c_im, d):
    B_, L, W = u.shape
    u = u.astype(jnp.float32).reshape(B_, L, S5_GROUPS, S5_GROUP_CH)
    dt = jnp.exp(log_dt.astype(jnp.float32))[:, None]
    ar = a_re.astype(jnp.float32)
    ai = a_im.astype(jnp.float32)
    mag = jnp.exp(dt * ar)
    abar_re = mag * jnp.cos(dt * ai)
    abar_im = mag * jnp.sin(dt * ai)
    den = ar * ar + ai * ai
    xr_ = abar_re - 1.0
    f_re = (xr_ * ar + abar_im * ai) / den
    f_im = (abar_im * ar - xr_ * ai) / den
    br = b_re.astype(jnp.float32)
    bi = b_im.astype(jnp.float32)
    bb_re = f_re[..., None] * br - f_im[..., None] * bi
    bb_im = f_re[..., None] * bi + f_im[..., None] * br
    bu_re = jnp.einsum('blgc,gpc->blgp', u, bb_re)
    bu_im = jnp.einsum('blgc,gpc->blgp', u, bb_im)
    x0r = x0_re.astype(jnp.float32)
    x0i = x0_im.astype(jnp.float32)
    bu_re = bu_re.at[:, 0].add(abar_re * x0r - abar_im * x0i)
    bu_im = bu_im.at[:, 0].add(abar_re * x0i + abar_im * x0r)
    a_full_re = jnp.broadcast_to(abar_re, bu_re.shape)
    a_full_im = jnp.broadcast_to(abar_im, bu_im.shape)
    _, _, xs_re, xs_im = lax.associative_scan(_complex_affine_combine, (a_full_re, a_full_im, bu_re, bu_im), axis=1)
    y = (jnp.einsum('blgp,gcp->blgc', xs_re, c_re.astype(jnp.float32))
         - jnp.einsum('blgp,gcp->blgc', xs_im, c_im.astype(jnp.float32))
         + d.astype(jnp.float32) * u)
    return y.reshape(B_, L, W), xs_re[:, -1].astype(x0_re.dtype), xs_im[:, -1].astype(x0_im.dtype)


def even_mixer(h, pos0, ret0, s5r0, s5i0, chunk, w_in_even, w_out_even, s5_a_re, s5_a_im, s5_log_dt,
               s5_b_re, s5_b_im, s5_c_re, s5_c_im, s5_d, w_glu, b_glu):
    B_, L, _ = h.shape
    q, k, v, g, u = split_cols(h @ w_in_even, [RET_WIDTH, RET_WIDTH, RET_WIDTH, RET_WIDTH, S5_WIDTH])
    pos = pos0 + jnp.arange(L)
    q = rotary(q.reshape(B_, L, RET_HEADS, RET_DK), pos)
    k = rotary(k.reshape(B_, L, RET_HEADS, RET_DK), pos) * (RET_DK ** -0.5)
    v = v.reshape(B_, L, RET_HEADS, RET_DV)
    o, S = retention(q, k, v, ret0, chunk)
    o = head_norm(o).reshape(B_, L, RET_WIDTH).astype(h.dtype) * jax.nn.silu(g)
    y, sr, si = s5_layer(u, s5r0, s5i0, s5_a_re, s5_a_im, s5_log_dt, s5_b_re, s5_b_im, s5_c_re, s5_c_im, s5_d)
    gl = jax.nn.gelu(y.astype(h.dtype))
    y = gl * jax.nn.sigmoid(gl @ w_glu + b_glu)
    out = jnp.concatenate([o, y], axis=-1) @ w_out_even
    return out, (S.astype(ret0.dtype), sr, si)


def odd_project(h, w_in_odd):
    B_, L, _ = h.shape
    q, k, v, qi, ki, wi = split_cols(h @ w_in_odd, [ATT_HEADS * ATT_HD, ATT_KV_HEADS * ATT_HD, ATT_KV_HEADS * ATT_HD,
                                                    IDX_HEADS * IDX_DIM, IDX_DIM, IDX_HEADS])
    return (q.reshape(B_, L, ATT_HEADS, ATT_HD), k.reshape(B_, L, ATT_KV_HEADS, ATT_HD),
            v.reshape(B_, L, ATT_KV_HEADS, ATT_HD), qi.reshape(B_, L, IDX_HEADS, IDX_DIM), ki, wi)


def index_select(qi, wi, ki, q_pos, topk):
    s = jax.nn.relu(jnp.einsum('bqhd,bsd->bqhs', qi, ki).astype(jnp.float32) * (IDX_DIM ** -0.5))
    score = jnp.einsum('bqhs,bqh->bqs', s, wi.astype(jnp.float32)) * (IDX_HEADS ** -0.5)
    k_pos = jnp.arange(ki.shape[1])
    score = jnp.where(k_pos[None, None, :] <= q_pos[None, :, None], score, -jnp.inf)
    _, sel = lax.top_k(score, topk)
    valid = sel <= q_pos[None, :, None]
    return sel, valid


def attend(q, ks, vs, valid):
    B_, Q, H, HD = q.shape
    nkv = ks.shape[-2]
    qg = q.reshape(B_, Q, nkv, H // nkv, HD)
    s = jnp.einsum('bqngd,bqknd->bqngk', qg, ks).astype(jnp.float32) * (HD ** -0.5)
    s = jnp.where(valid[:, :, None, None, :], s, -jnp.inf)
    p = jax.nn.softmax(s, axis=-1).astype(vs.dtype)
    o = jnp.einsum('bqngk,bqknd->bqngd', p, vs)
    return o.reshape(B_, Q, H * HD)


def odd_mixer_prompt(h, w_in_odd, w_out_odd):
    B_, L, _ = h.shape
    q, k, v, qi, ki, wi = odd_project(h, w_in_odd)
    qb = min(ATT_Q_BLOCK, L)
    nb = L // qb
    topk = min(INDEX_TOPK, L // 4)
    bidx = jnp.arange(B_)[:, None, None]

    def block(i):
        start = i * qb
        sl = lambda t: lax.dynamic_slice_in_dim(t, start, qb, axis=1)
        q_pos = start + jnp.arange(qb)
        sel, valid = index_select(sl(qi), sl(wi), ki, q_pos, topk)
        return attend(sl(q), k[bidx, sel], v[bidx, sel], valid)

    o = lax.map(block, jnp.arange(nb))
    o = o.transpose(1, 0, 2, 3).reshape(B_, L, ATT_HEADS * ATT_HD)
    return o @ w_out_odd, (k, v, ki)


def odd_mixer_sample(h, cache_k, cache_v, cache_kidx, page_table, w_in_odd, w_out_odd):
    B_, L, _ = h.shape
    q, k, v, qi, ki, wi = odd_project(h, w_in_odd)
    past = page_table.shape[1] * PAGE_SIZE
    ki_past = cache_kidx[page_table].reshape(B_, past, IDX_DIM)
    ki_all = jnp.concatenate([ki_past, ki.astype(ki_past.dtype)], axis=1)
    q_pos = past + jnp.arange(L)
    topk = min(INDEX_TOPK, (past + L) // 4)
    sel, valid = index_select(qi, wi, ki_all, q_pos, topk)
    bidx = jnp.arange(B_)[:, None, None]
    in_past = (sel < past)[..., None, None]
    ps = jnp.minimum(sel, past - 1)
    phys = page_table[bidx, ps // PAGE_SIZE]
    off = ps % PAGE_SIZE
    new_i = jnp.clip(sel - past, 0, L - 1)
    ks = jnp.where(in_past, cache_k[phys, off], k[bidx, new_i])
    vs = jnp.where(in_past, cache_v[phys, off], v[bidx, new_i])
    o = attend(q, ks, vs, valid)
    return o @ w_out_odd, (k, v, ki)


def trunk(x, c, even_fn, odd_fn, norm_g, w_ada, b_ada, w_ffn_in, w_ffn_out, final_g):
    B_ = x.shape[0]
    states = []
    for layer in range(DEPTH):
        mod = (jax.nn.silu(c) @ w_ada[layer] + b_ada[layer]).reshape(B_, N_SUBLAYERS, 3, 1, D_MODEL)
        h = rmsnorm(x, norm_g[layer, 0]) * (1.0 + mod[:, 0, 1]) + mod[:, 0, 0]
        x = x + FFN_RES * mod[:, 0, 2] * swiglu(h, w_ffn_in[layer, 0], w_ffn_out[layer, 0])
        h = rmsnorm(x, norm_g[layer, 1]) * (1.0 + mod[:, 1, 1]) + mod[:, 1, 0]
        mix_fn = even_fn if layer % 2 == 0 else odd_fn
        y, st = mix_fn(h)
        x = x + mod[:, 1, 2] * y
        states.append(st)
        h = rmsnorm(x, norm_g[layer, 2]) * (1.0 + mod[:, 2, 1]) + mod[:, 2, 0]
        x = x + FFN_RES * mod[:, 2, 2] * swiglu(h, w_ffn_in[layer, 1], w_ffn_out[layer, 1])
    return rmsnorm(x, final_g), states


def setup_inputs(seed: int = 0) -> dict:
    key = jax.random.key(seed)
    ks = iter(jax.random.split(key, 48))
    f32 = jnp.float32
    nrm = lambda shape, scale: jax.random.normal(next(ks), shape, f32) * scale
    n_pages = PAST_LEN // PAGE_SIZE
    n_phys = (DEC_BATCH * n_pages * PAGE_POOL_NUM) // PAGE_POOL_DEN
    page_table = jax.random.permutation(next(ks), n_phys)[: DEC_BATCH * n_pages].reshape(DEC_BATCH, n_pages).astype(jnp.int32)
    return {
        'x_prompt': nrm((BATCH, SEQ, D_MODEL), 1.0),
        'x_sample': nrm((DEC_BATCH, DEC_SEQ, D_MODEL), 1.0),
        'c_prompt': nrm((BATCH, D_MODEL), 1.0),
        'c_sample': nrm((DEC_BATCH, D_MODEL), 1.0),
        'state_ret': nrm((DEC_BATCH, RET_HEADS, RET_DK, RET_DV), 0.1),
        'state_s5_re': nrm((DEC_BATCH, S5_GROUPS, S5_STATE), 0.3),
        'state_s5_im': nrm((DEC_BATCH, S5_GROUPS, S5_STATE), 0.3),
        'cache_k': nrm((n_phys, PAGE_SIZE, ATT_KV_HEADS, ATT_HD), 1.0),
        'cache_v': nrm((n_phys, PAGE_SIZE, ATT_KV_HEADS, ATT_HD), 1.0),
        'cache_kidx': nrm((n_phys, PAGE_SIZE, IDX_DIM), 1.0),
        'page_table': page_table,
        'norm_g': 1.0 + nrm((DEPTH, N_SUBLAYERS, D_MODEL), 0.01),
        'w_ada': nrm((DEPTH, D_MODEL, N_SUBLAYERS * 3 * D_MODEL), D_MODEL ** -0.5),
        'b_ada': nrm((DEPTH, N_SUBLAYERS * 3 * D_MODEL), 0.01),
        'w_ffn_in': nrm((DEPTH, 2, D_MODEL, 2 * D_FF), D_MODEL ** -0.5),
        'w_ffn_out': nrm((DEPTH, 2, D_FF, D_MODEL), D_FF ** -0.5),
        'w_in_even': nrm((D_MODEL, EVEN_IN), D_MODEL ** -0.5),
        'w_out_even': nrm((MIX_WIDTH, D_MODEL), MIX_WIDTH ** -0.5),
        's5_a_re': -0.5 + nrm((S5_GROUPS, S5_STATE), 0.01),
        's5_a_im': jnp.pi * jnp.arange(S5_STATE, dtype=f32)[None, :] + nrm((S5_GROUPS, S5_STATE), 0.01),
        's5_log_dt': jax.random.uniform(next(ks), (S5_GROUPS,), f32, math.log(1e-3), math.log(1e-1)),
        's5_b_re': nrm((S5_GROUPS, S5_STATE, S5_GROUP_CH), S5_GROUP_CH ** -0.5),
        's5_b_im': nrm((S5_GROUPS, S5_STATE, S5_GROUP_CH), S5_GROUP_CH ** -0.5),
        's5_c_re': nrm((S5_GROUPS, S5_GROUP_CH, S5_STATE), S5_STATE ** -0.5),
        's5_c_im': nrm((S5_GROUPS, S5_GROUP_CH, S5_STATE), S5_STATE ** -0.5),
        's5_d': nrm((S5_GROUPS, S5_GROUP_CH), 1.0),
        'w_glu': nrm((S5_WIDTH, S5_WIDTH), S5_WIDTH ** -0.5),
        'b_glu': nrm((S5_WIDTH,), 0.01),
        'w_in_odd': nrm((D_MODEL, ODD_IN), D_MODEL ** -0.5),
        'w_out_odd': nrm((ATT_HEADS * ATT_HD, D_MODEL), (ATT_HEADS * ATT_HD) ** -0.5),
        'final_g': 1.0 + nrm((D_MODEL,), 0.01),
    }


def reference(x_prompt, x_sample, c_prompt, c_sample, state_ret, state_s5_re, state_s5_im, cache_k, cache_v,
              cache_kidx, page_table, norm_g, w_ada, b_ada, w_ffn_in, w_ffn_out, w_in_even, w_out_even,
              s5_a_re, s5_a_im, s5_log_dt, s5_b_re, s5_b_im, s5_c_re, s5_c_im, s5_d, w_glu, b_glu,
              w_in_odd, w_out_odd, final_g):
    s5_params = (s5_a_re, s5_a_im, s5_log_dt, s5_b_re, s5_b_im, s5_c_re, s5_c_im, s5_d, w_glu, b_glu)
    bp, lp = x_prompt.shape[:2]
    ls = x_sample.shape[1]
    zero_ret = jnp.zeros((bp, RET_HEADS, RET_DK, RET_DV), x_prompt.dtype)
    zero_s5 = jnp.zeros((bp, S5_GROUPS, S5_STATE), x_prompt.dtype)

    y_prompt, (ev_p, od_p) = trunk(
        x_prompt, c_prompt,
        lambda h: even_mixer(h, 0, zero_ret, zero_s5, zero_s5, min(RET_CHUNK, lp), w_in_even, w_out_even, *s5_params),
        lambda h: odd_mixer_prompt(h, w_in_odd, w_out_odd),
        norm_g, w_ada, b_ada, w_ffn_in, w_ffn_out, final_g)

    y_sample, (ev_s, od_s) = trunk(
        x_sample, c_sample,
        lambda h: even_mixer(h, PAST_LEN, state_ret, state_s5_re, state_s5_im, ls, w_in_even, w_out_even, *s5_params),
        lambda h: odd_mixer_sample(h, cache_k, cache_v, cache_kidx, page_table, w_in_odd, w_out_odd),
        norm_g, w_ada, b_ada, w_ffn_in, w_ffn_out, final_g)

    ret_p, s5re_p, s5im_p = ev_p
    k_p, v_p, kidx_p = od_p
    ret_s, s5re_s, s5im_s = ev_s
    k_s, v_s, kidx_s = od_s
    return (y_prompt, y_sample, ret_p, s5re_p, s5im_p, k_p, v_p, kidx_p, ret_s, s5re_s, s5im_s, k_s, v_s, kidx_s)
```

```python
import functools
import math

import jax
import jax.numpy as jnp
from jax import lax
from jax.experimental import pallas as pl
from jax.experimental.pallas import tpu as pltpu

F32 = jnp.float32
BF16 = jnp.bfloat16

EPS = 1e-6
FFN_RES = 0.5
N_SUBLAYERS = 3
RET_HEADS = 4
RET_CHUNK = 128
ROPE_BASE = 10000.0
S5_GROUP_CH = 16
S5_STATE = 64
ATT_HD = 128
ATT_KV_HEADS = 4
IDX_HEADS = 16
IDX_DIM = 128
INDEX_TOPK = 256
PAGE_SIZE = 128

LANES = 128
MIB = 1024 * 1024
NEG_INF = float("-inf")


def _params(sem, vmem_mib):
    return pltpu.CompilerParams(dimension_semantics=sem, vmem_limit_bytes=vmem_mib * MIB)


def _bdot(a, b):
    return jnp.dot(a.astype(BF16), b.astype(BF16), preferred_element_type=F32)


def _bdot_nt(a, b):
    return lax.dot_general(a.astype(BF16), b.astype(BF16), (((1,), (1,)), ((), ())),
                           preferred_element_type=F32)


def _bdot_tn(a, b):
    return lax.dot_general(a.astype(BF16), b.astype(BF16), (((0,), (0,)), ((), ())),
                           preferred_element_type=F32)


def _silu(x):
    return x * jax.nn.sigmoid(x)


def _rms(x, g):
    ms = jnp.mean(x * x, axis=-1, keepdims=True)
    return x * lax.rsqrt(ms + EPS) * g


def _row_chunks(n_rows, chunk, fn):
    if n_rows <= chunk:
        fn(slice(0, n_rows))
        return
    assert n_rows % chunk == 0

    def body(i, c):
        fn(pl.ds(pl.multiple_of(i * chunk, chunk), chunk))
        return c

    lax.fori_loop(0, n_rows // chunk, body, 0)


ROW_CHUNK = 256


def _norm_mod_to(h_ref, x_ref, g_ref, sc_ref, sh_ref):
    n_rows = x_ref.shape[0]
    assert sc_ref.shape[0] == 1 or n_rows <= ROW_CHUNK

    def fn(rows):
        y = _rms(x_ref[rows, :], g_ref[...])
        h_ref[rows, :] = (y * (1.0 + sc_ref[...]) + sh_ref[...]).astype(BF16)

    _row_chunks(n_rows, ROW_CHUNK, fn)


def _ada_kernel(c_ref, w_ref, b_ref, o_ref):
    o_ref[...] = _bdot(_silu(c_ref[...]), w_ref[...]) + b_ref[...]


def _ada_mod(c_all, w_ada, b_ada):
    depth, d, n = w_ada.shape
    rows = c_all.shape[0]
    tn = 1024
    return pl.pallas_call(
        _ada_kernel,
        grid=(depth, n // tn),
        in_specs=[
            pl.BlockSpec((rows, d), lambda l, j: (0, 0)),
            pl.BlockSpec((None, d, tn), lambda l, j: (l, 0, j)),
            pl.BlockSpec((None, 1, tn), lambda l, j: (l, 0, j)),
        ],
        out_specs=pl.BlockSpec((None, rows, tn), lambda l, j: (l, 0, j)),
        out_shape=jax.ShapeDtypeStruct((depth, rows, n), F32),
        compiler_params=_params(("arbitrary", "arbitrary"), 40),
        name="ada_mod",
    )(c_all, w_ada, b_ada.reshape(depth, 1, n))


def _mod_specs(mod, sub, tiles_per_batch, which):
    _, _, r, d = mod.shape
    specs = []
    for k in which:
        idx = sub * 3 + k
        specs.append(pl.BlockSpec((None, None, r, d),
                                  lambda i, j, idx=idx: (i // tiles_per_batch, idx, 0, 0)))
    return specs


def _ffn_kernel(x_ref, g_ref, sh_ref, sc_ref, gt_ref, wa_ref, wb_ref, wo_ref, fg_ref, o_ref, h_ref,
                *, final_norm):
    j = pl.program_id(1)
    nj = pl.num_programs(1)

    @pl.when(j == 0)
    def _():
        _norm_mod_to(h_ref, x_ref, g_ref, sc_ref, sh_ref)

    h = h_ref[...]
    a = jnp.dot(h, wa_ref[...].astype(BF16), preferred_element_type=F32)
    b = jnp.dot(h, wb_ref[...].astype(BF16), preferred_element_type=F32)
    p = _bdot(_silu(a) * b, wo_ref[...])

    @pl.when(j == 0)
    def _():
        o_ref[...] = p

    @pl.when(j > 0)
    def _():
        o_ref[...] += p

    @pl.when(j == nj - 1)
    def _():
        def fn(rows):
            y = x_ref[rows, :] + (FFN_RES * gt_ref[...]) * o_ref[rows, :]
            if final_norm:
                y = _rms(y, fg_ref[...])
            o_ref[rows, :] = y

        _row_chunks(x_ref.shape[0], ROW_CHUNK, fn)


def _ffn(x2d, norm_g, mod, sub, w_in, w_out, wsel, final_g, *, tm, tf, tiles_per_batch, final_norm, vmem_mib):
    m, d = x2d.shape
    f = w_out.shape[2]
    nf = f // tf
    assert m % tm == 0 and f % tf == 0
    row = lambda i, j: (0, 0)
    l, k = wsel
    return pl.pallas_call(
        functools.partial(_ffn_kernel, final_norm=final_norm),
        grid=(m // tm, nf),
        in_specs=[
            pl.BlockSpec((tm, d), lambda i, j: (i, 0)),
            pl.BlockSpec((1, d), row),
            *_mod_specs(mod, sub, tiles_per_batch, (0, 1, 2)),
            pl.BlockSpec((None, None, d, tf), lambda i, j: (l, k, 0, j)),
            pl.BlockSpec((None, None, d, tf), lambda i, j: (l, k, 0, nf + j)),
            pl.BlockSpec((None, None, tf, d), lambda i, j: (l, k, j, 0)),
            pl.BlockSpec((1, d), row),
        ],
        out_specs=pl.BlockSpec((tm, d), lambda i, j: (i, 0)),
        out_shape=jax.ShapeDtypeStruct((m, d), F32),
        scratch_shapes=[pltpu.VMEM((tm, d), BF16)],
        compiler_params=_params(("parallel", "arbitrary"), vmem_mib),
        name="ffn",
    )(x2d, norm_g.reshape(1, d), mod, mod, mod, w_in, w_in, w_out, final_g.reshape(1, d))


def _proj_kernel(x_ref, g_ref, sh_ref, sc_ref, w_ref, o_ref, h_ref):
    @pl.when(pl.program_id(1) == 0)
    def _():
        _norm_mod_to(h_ref, x_ref, g_ref, sc_ref, sh_ref)

    o_ref[...] = jnp.dot(h_ref[...], w_ref[...].astype(BF16), preferred_element_type=F32)


def _proj(x2d, norm_g, mod, sub, w, *, tm, tn, tiles_per_batch, vmem_mib):
    m, d = x2d.shape
    n = w.shape[1]
    assert m % tm == 0 and n % tn == 0
    return pl.pallas_call(
        _proj_kernel,
        grid=(m // tm, n // tn),
        in_specs=[
            pl.BlockSpec((tm, d), lambda i, j: (i, 0)),
            pl.BlockSpec((1, d), lambda i, j: (0, 0)),
            *_mod_specs(mod, sub, tiles_per_batch, (0, 1)),
            pl.BlockSpec((d, tn), lambda i, j: (0, j)),
        ],
        out_specs=pl.BlockSpec((tm, tn), lambda i, j: (i, j)),
        out_shape=jax.ShapeDtypeStruct((m, n), F32),
        scratch_shapes=[pltpu.VMEM((tm, d), BF16)],
        compiler_params=_params(("parallel", "arbitrary"), vmem_mib),
        name="mixer_in_proj",
    )(x2d, norm_g.reshape(1, d), mod, mod, w)


def _outproj_kernel(*refs, n_parts):
    a_refs = refs[:n_parts]
    w_ref, x_ref, gt_ref, o_ref = refs[n_parts:]
    acc = None
    k0 = 0
    for a_ref in a_refs:
        kk = a_ref.shape[1]
        p = jnp.dot(a_ref[...], w_ref[k0:k0 + kk, :].astype(BF16), preferred_element_type=F32)
        acc = p if acc is None else acc + p
        k0 += kk
    o_ref[...] = x_ref[...] + gt_ref[...] * acc


def _outproj(a_parts, w, x2d, mod, *, tm, tn, tiles_per_batch, vmem_mib):
    m, d = x2d.shape
    k = w.shape[0]
    assert sum(a.shape[1] for a in a_parts) == k and m % tm == 0 and d % tn == 0
    _, _, r, _ = mod.shape
    return pl.pallas_call(
        functools.partial(_outproj_kernel, n_parts=len(a_parts)),
        grid=(m // tm, d // tn),
        in_specs=[
            *[pl.BlockSpec((tm, a.shape[1]), lambda i, j: (i, 0)) for a in a_parts],
            pl.BlockSpec((k, tn), lambda i, j: (0, j)),
            pl.BlockSpec((tm, tn), lambda i, j: (i, j)),
            pl.BlockSpec((None, None, r, tn), lambda i, j: (i // tiles_per_batch, 5, 0, j)),
        ],
        out_specs=pl.BlockSpec((tm, tn), lambda i, j: (i, j)),
        out_shape=jax.ShapeDtypeStruct((m, d), F32),
        compiler_params=_params(("parallel", "arbitrary"), vmem_mib),
        name="mixer_out_proj",
    )(*a_parts, w, x2d, mod)


def _rot(x1, x2, cos, sin):
    return jnp.concatenate([x1 * cos - x2 * sin, x1 * sin + x2 * cos], axis=-1)


def _head_norm(o):
    mu = jnp.mean(o, axis=-1, keepdims=True)
    var = jnp.mean(jnp.square(o - mu), axis=-1, keepdims=True)
    return (o - mu) * lax.rsqrt(var + 1e-5)


def _ret_log_g(h):
    return math.log1p(-(2.0 ** (-5.0 - h)))


def _ret_kernel(q_ref, k_ref, v_ref, g_ref, cos_ref, sin_ref, s0_ref, o_ref, s_ref, *, chunk):
    c = pl.program_id(1)
    rb = q_ref.shape[0]
    dk = q_ref.shape[1] // RET_HEADS
    half = dk // 2

    @pl.when(c == 0)
    def _():
        s_ref[...] = s0_ref[...]

    ri = lax.broadcasted_iota(jnp.int32, (chunk, chunk), 0).astype(F32)
    ci = lax.broadcasted_iota(jnp.int32, (chunk, chunk), 1).astype(F32)
    diff = ri - ci
    rowf = lax.broadcasted_iota(jnp.int32, (chunk, dk), 0).astype(F32)
    for h in range(RET_HEADS):
        lg = _ret_log_g(h)
        dmat = jnp.where(diff >= 0, jnp.exp(jnp.maximum(diff, 0.0) * lg), 0.0)
        q_dec = jnp.exp((rowf + 1.0) * lg)
        k_dec = jnp.exp((chunk - 1.0 - rowf) * lg)
        c_dec = math.exp(chunk * lg)
        c0 = h * dk
        for ck in range(rb // chunk):
            rows = slice(ck * chunk, (ck + 1) * chunk)
            cos = cos_ref[rows, :]
            sin = sin_ref[rows, :]
            qr = _rot(q_ref[rows, c0:c0 + half], q_ref[rows, c0 + half:c0 + dk], cos, sin)
            kr = _rot(k_ref[rows, c0:c0 + half], k_ref[rows, c0 + half:c0 + dk], cos, sin) * (dk ** -0.5)
            vb = v_ref[rows, c0:c0 + dk].astype(BF16)
            qb = qr.astype(BF16)
            a = _bdot_nt(qb, kr) * dmat
            s = s_ref[h]
            o = _bdot(a, vb) + _bdot(qb, s) * q_dec
            s_ref[h] = s * c_dec + _bdot_tn(kr * k_dec, vb)
            o_ref[rows, c0:c0 + dk] = (_head_norm(o) * _silu(g_ref[rows, c0:c0 + dk])).astype(BF16)


def _retention(z2d, col0, cos, sin, state0, *, batch, seq, rb, chunk):
    _, h, dk, dv = state0.shape
    w = h * dk
    nb = seq // rb
    assert seq % rb == 0 and rb % chunk == 0 and col0 % w == 0
    cb = col0 // w
    zspec = lambda k: pl.BlockSpec((rb, w), lambda b, c, k=k: (b * nb + c, cb + k))
    tab = pl.BlockSpec((rb, dk // 2), lambda b, c: (c, 0))
    st = pl.BlockSpec((None, h, dk, dv), lambda b, c: (b, 0, 0, 0))
    return pl.pallas_call(
        functools.partial(_ret_kernel, chunk=chunk),
        grid=(batch, nb),
        in_specs=[zspec(0), zspec(1), zspec(2), zspec(3), tab, tab, st],
        out_specs=[pl.BlockSpec((rb, w), lambda b, c: (b * nb + c, 0)), st],
        out_shape=[jax.ShapeDtypeStruct((batch * seq, w), BF16),
                   jax.ShapeDtypeStruct(state0.shape, F32)],
        compiler_params=_params(("parallel", "arbitrary"), 40),
        name="retention",
    )(z2d, z2d, z2d, z2d, cos, sin, state0)


def _rope_tables(pos, half):
    freqs = ROPE_BASE ** (-jnp.arange(half, dtype=F32) / half)
    ang = pos.astype(F32)[:, None] * freqs[None, :]
    return jnp.cos(ang), jnp.sin(ang)


def _r16(x):
    return x.astype(BF16).astype(F32)


def _ret_step_kernel(q_ref, krow_ref, kcol_ref, v_ref, g_ref, cos_ref, sin_ref, cosc_ref, sinc_ref, s0_ref,
                     o_ref, s_ref):
    dk = q_ref.shape[1] // RET_HEADS
    half = dk // 2
    cos, sin = cos_ref[...], sin_ref[...]
    cosc, sinc = cosc_ref[...], sinc_ref[...]
    for h in range(RET_HEADS):
        dec = math.exp(_ret_log_g(h))
        c0 = h * dk
        qr = _r16(_rot(q_ref[:, c0:c0 + half], q_ref[:, c0 + half:c0 + dk], cos, sin))
        kr = _r16(_rot(krow_ref[:, c0:c0 + half], krow_ref[:, c0 + half:c0 + dk], cos, sin) * (dk ** -0.5))
        k1, k2 = kcol_ref[h, 0:half, :], kcol_ref[h, half:dk, :]
        kc = _r16(jnp.concatenate([k1 * cosc - k2 * sinc, k1 * sinc + k2 * cosc], axis=0) * (dk ** -0.5))
        v = _r16(v_ref[:, c0:c0 + dk])
        a = jnp.sum(qr * kr, axis=-1, keepdims=True)
        s = s0_ref[h]
        o = _r16(a) * v + _bdot(qr, s) * dec
        s_ref[h] = s * dec + kc * v
        o_ref[:, c0:c0 + dk] = (_head_norm(o) * _silu(g_ref[:, c0:c0 + dk])).astype(BF16)


def _retention_step(z2d, cos, sin, state0):
    b, h, dk, dv = state0.shape
    w = h * dk
    z3 = z2d.reshape(b, 1, z2d.shape[1])
    kcol = z2d[:, w:2 * w].reshape(b, h, dk, 1)
    half = dk // 2
    zspec = lambda k: pl.BlockSpec((None, 1, w), lambda i, k=k: (i, 0, k))
    row = pl.BlockSpec((1, half), lambda i: (0, 0))
    col = pl.BlockSpec((half, 1), lambda i: (0, 0))
    st = pl.BlockSpec((None, h, dk, dv), lambda i: (i, 0, 0, 0))
    og, s = pl.pallas_call(
        _ret_step_kernel,
        grid=(b,),
        in_specs=[zspec(0), zspec(1), pl.BlockSpec((None, h, dk, 1), lambda i: (i, 0, 0, 0)), zspec(2), zspec(3),
                  row, row, col, col, st],
        out_specs=[pl.BlockSpec((None, 1, w), lambda i: (i, 0, 0)), st],
        out_shape=[jax.ShapeDtypeStruct((b, 1, w), BF16), jax.ShapeDtypeStruct(state0.shape, F32)],
        compiler_params=_params(("parallel",), 32),
        name="retention_step",
    )(z3, z3, kcol, z3, z3, cos, sin, cos.reshape(half, 1), sin.reshape(half, 1), state0)
    return og.reshape(b, w), s


S5_BLOCKS = 8


def _s5_disc(ar, ai, dt):
    mag = jnp.exp(dt * ar)
    abr = mag * jnp.cos(dt * ai)
    abi = mag * jnp.sin(dt * ai)
    den = ar * ar + ai * ai
    xr = abr - 1.0
    return abr, abi, (xr * ar + abi * ai) / den, (abi * ar - xr * ai) / den


def _s5_param_kernel(ar_ref, ai_ref, ldt_ref, arr_ref, air_ref, br_ref, bi_ref, abr_ref, abi_ref, bbr_ref, bbi_ref):
    dt = jnp.exp(ldt_ref[...])
    abr_ref[...], abi_ref[...], _, _ = _s5_disc(ar_ref[...], ai_ref[...], dt)
    _, _, fr, fi = _s5_disc(arr_ref[...], air_ref[...], dt)
    br, bi = br_ref[...], bi_ref[...]
    bbr_ref[...] = fr * br - fi * bi
    bbi_ref[...] = fr * bi + fi * br


def _s5_prepare(p):
    g, st = p["s5_a_re"].shape
    ch = p["s5_d"].shape[1]
    rep = lambda a: jnp.repeat(a, ch, axis=1)
    abr, abi, bbr, bbi = pl.pallas_call(
        _s5_param_kernel,
        out_shape=[jax.ShapeDtypeStruct((g, st), F32)] * 2 + [jax.ShapeDtypeStruct((g, st * ch), F32)] * 2,
        name="s5_discretise",
    )(p["s5_a_re"], p["s5_a_im"], p["s5_log_dt"].reshape(g, 1), rep(p["s5_a_re"]), rep(p["s5_a_im"]),
      p["s5_b_re"].reshape(g, st * ch), p["s5_b_im"].reshape(g, st * ch))
    nb = S5_BLOCKS
    gl = g // nb
    eye = jnp.eye(gl, dtype=F32)
    w_in = lambda bb: jnp.einsum("bgpc,gh->bgchp", bb.reshape(nb, gl, st, ch), eye).reshape(nb, gl * ch, gl * st)
    w_out = lambda c: jnp.einsum("bgcp,gh->bgphc", c.reshape(nb, gl, ch, st), eye).reshape(nb, gl * st, gl * ch)
    return dict(
        abr=abr.reshape(1, g * st), abi=abi.reshape(1, g * st),
        wbr=w_in(bbr).astype(BF16), wbi=w_in(bbi).astype(BF16),
        wcr=w_out(p["s5_c_re"]).astype(BF16), wci=w_out(p["s5_c_im"]).astype(BF16),
        d=p["s5_d"].reshape(1, g * ch), w_glu=p["w_glu"], b_glu=p["b_glu"].reshape(1, -1))


S5_SCAN_LANES = 1024


def _s5_kernel(u_ref, x0r_ref, x0i_ref, abr_ref, abi_ref, wbr_ref, wbi_ref, wcr_ref, wci_ref, d_ref, wg_ref,
               bg_ref, y_ref, xr_ref, xi_ref, bur, bui, yb, wgb, *, ns, t):
    nblk = wbr_ref.shape[0]
    uw, sw = wbr_ref.shape[1], wbr_ref.shape[2]
    n_state = nblk * sw
    seqs = range(ns) if u_ref.ndim == 3 else [None]
    rows_of = lambda s: slice(0, ns) if s is None else slice(s * t, (s + 1) * t)
    get_u = lambda s, cols: u_ref[:, cols] if s is None else u_ref[s, :, cols]

    @pl.when(pl.program_id(0) == 0)
    def _():
        xr_ref[...] = x0r_ref[...]
        xi_ref[...] = x0i_ref[...]
        wgb[...] = wg_ref[...].astype(BF16)

    tiles_per_blk = sw // LANES
    lane_tile = lambda j: slice(j * LANES, (j + 1) * LANES)
    for s in seqs:
        for b in range(nblk):
            ub = get_u(s, slice(b * uw, (b + 1) * uw)).astype(BF16)
            pr = jnp.dot(ub, wbr_ref[b], preferred_element_type=F32)
            pi = jnp.dot(ub, wbi_ref[b], preferred_element_type=F32)
            for k in range(tiles_per_blk):
                bur[b * tiles_per_blk + k, rows_of(s), :] = pr[:, lane_tile(k)]
                bui[b * tiles_per_blk + k, rows_of(s), :] = pi[:, lane_tile(k)]

    per_pass = S5_SCAN_LANES // LANES
    for j0 in range(0, n_state // LANES, per_pass):
        tiles = list(range(j0, min(j0 + per_pass, n_state // LANES)))
        ar = [jnp.broadcast_to(abr_ref[:, lane_tile(j)], (ns, LANES)) for j in tiles]
        ai = [jnp.broadcast_to(abi_ref[:, lane_tile(j)], (ns, LANES)) for j in tiles]

        def step(tt, carry, tiles=tiles, ar=ar, ai=ai):
            idx = pl.ds(tt, ns, stride=t) if t > 1 else slice(0, ns)
            out = []
            for k, j in enumerate(tiles):
                xr, xi = carry[k]
                nr = ar[k] * xr - ai[k] * xi + bur[j, idx, :]
                ni = ar[k] * xi + ai[k] * xr + bui[j, idx, :]
                bur[j, idx, :] = nr
                bui[j, idx, :] = ni
                out.append((nr, ni))
            return tuple(out)

        fin = lax.fori_loop(0, t, step, tuple((xr_ref[:, lane_tile(j)], xi_ref[:, lane_tile(j)]) for j in tiles))
        for k, j in enumerate(tiles):
            xr_ref[:, lane_tile(j)], xi_ref[:, lane_tile(j)] = fin[k]

    blk_states = lambda ref, b: jnp.concatenate(
        [ref[b * tiles_per_blk + k] for k in range(tiles_per_blk)], axis=1).astype(BF16)
    for b in range(nblk):
        yb[:, b * uw:(b + 1) * uw] = (jnp.dot(blk_states(bur, b), wcr_ref[b], preferred_element_type=F32)
                                      - jnp.dot(blk_states(bui, b), wci_ref[b], preferred_element_type=F32))
    for s in seqs:
        yb[rows_of(s), :] = jax.nn.gelu(yb[rows_of(s), :] + d_ref[...] * get_u(s, slice(None)))
    gate = jax.nn.sigmoid(jnp.dot(yb[...].astype(BF16), wgb[...], preferred_element_type=F32) + bg_ref[...])
    for s in seqs:
        val = (yb[rows_of(s), :] * gate[rows_of(s), :]).astype(BF16)
        if s is None:
            y_ref[...] = val
        else:
            y_ref[s] = val


def _s5(z2d, col_blk, sp, x0r, x0i, *, batch, seq, t):
    w = sp["d"].shape[1]
    n_state = sp["abr"].shape[1]
    if seq == 1:
        ns, grid, rows = batch, (1,), batch
        u = z2d
        u_spec = pl.BlockSpec((batch, w), lambda c: (0, col_blk))
        y_spec = pl.BlockSpec((batch, w), lambda c: (0, 0))
        y_shape = jax.ShapeDtypeStruct((batch, w), BF16)
    else:
        assert seq % t == 0
        ns, grid, rows = batch, (seq // t,), batch * t
        u = z2d.reshape(batch, seq, z2d.shape[1])
        u_spec = pl.BlockSpec((batch, t, w), lambda c: (0, c, col_blk))
        y_spec = pl.BlockSpec((batch, t, w), lambda c: (0, c, 0))
        y_shape = jax.ShapeDtypeStruct((batch, seq, w), BF16)
    full = lambda a: pl.BlockSpec(a.shape, lambda c, nd=a.ndim: (0,) * nd)
    st_spec = pl.BlockSpec((batch, n_state), lambda c: (0, 0))
    consts = [sp["abr"], sp["abi"], sp["wbr"], sp["wbi"], sp["wcr"], sp["wci"], sp["d"], sp["w_glu"], sp["b_glu"]]
    y, xr, xi = pl.pallas_call(
        functools.partial(_s5_kernel, ns=ns, t=t if seq > 1 else 1),
        grid=grid,
        in_specs=[u_spec, st_spec, st_spec] + [full(a) for a in consts],
        out_specs=[y_spec, st_spec, st_spec],
        out_shape=[y_shape, jax.ShapeDtypeStruct((batch, n_state), F32), jax.ShapeDtypeStruct((batch, n_state), F32)],
        scratch_shapes=[pltpu.VMEM((n_state // LANES, rows, LANES), F32), pltpu.VMEM((n_state // LANES, rows, LANES), F32),
                        pltpu.VMEM((rows, w), F32), pltpu.VMEM(sp["w_glu"].shape, BF16)],
        compiler_params=_params(("arbitrary",), 48),
        name="s5",
    )(u, x0r.reshape(batch, n_state), x0i.reshape(batch, n_state), *consts)
    return y.reshape(batch * seq, w), xr, xi


def _even_layer(x2d, norm_g, mod, p, ret0, s5r0, s5i0, *, batch, seq, pos0, tm, tiles_per_batch, rb, chunk, s5_t,
                sp=None):
    _, h, dk, _ = ret0.shape
    rw = h * dk
    z = _proj(x2d, norm_g, mod, 1, p["w_in_even"], tm=tm, tn=1024, tiles_per_batch=tiles_per_batch, vmem_mib=52)
    cos, sin = _rope_tables(pos0 + jnp.arange(seq), dk // 2)
    if seq > 1:
        og, s_ret = _retention(z, 0, cos, sin, ret0, batch=batch, seq=seq, rb=rb, chunk=chunk)
    else:
        og, s_ret = _retention_step(z, cos, sin, ret0)
    sp = _s5_prepare(p) if sp is None else sp
    y, sr, si = _s5(z, (4 * rw) // sp["d"].shape[1], sp, s5r0, s5i0, batch=batch, seq=seq, t=s5_t)
    x_new = _outproj([og, y], p["w_out_even"], x2d, mod, tm=tm, tn=1024, tiles_per_batch=tiles_per_batch,
                     vmem_mib=48)
    return x_new, s_ret, sr.reshape(s5r0.shape), si.reshape(s5i0.shape)


INT32_MIN = -2 ** 31


def _sortable_keys(score):
    score = jnp.where(score == 0.0, 0.0, score)
    bits = lax.bitcast_convert_type(score, jnp.int32)
    return bits ^ (lax.shift_right_arithmetic(bits, 31) & 0x7FFFFFFF)


def _kth_largest(count_ge, shape, k):
    kf = jnp.float32(k)
    prefix = jnp.where(count_ge(jnp.zeros(shape, jnp.int32)) >= kf, 0, INT32_MIN).astype(jnp.int32)

    def body(i, prefix):
        cand = prefix | lax.shift_left(jnp.int32(1), 30 - i)
        return jnp.where(count_ge(cand) >= kf, cand, prefix)

    return lax.fori_loop(0, 31, body, prefix)


def _odd_columns(d_model):
    n_q = d_model
    n_kv = ATT_KV_HEADS * ATT_HD
    n_qi = IDX_HEADS * IDX_DIM
    off = dict(q=0, qi=n_q, k=n_q + n_qi, v=n_q + n_qi + n_kv, ki=n_q + n_qi + 2 * n_kv)
    off["wi"] = off["ki"] + IDX_DIM
    off["end"] = off["wi"] + LANES
    return off


def _odd_weight(w_in_odd, d_model):
    n_kv = ATT_KV_HEADS * ATT_HD
    n_qi = IDX_HEADS * IDX_DIM
    cuts = [d_model, d_model + n_kv, d_model + 2 * n_kv, d_model + 2 * n_kv + n_qi, d_model + 2 * n_kv + n_qi + IDX_DIM]
    q, k, v, qi, ki, wi = jnp.split(w_in_odd, cuts, axis=1)
    pad = jnp.zeros((w_in_odd.shape[0], LANES - wi.shape[1]), w_in_odd.dtype)
    return jnp.concatenate([q, qi, k, v, ki, wi, pad], axis=1)


def _dsa_kernel(q_ref, qi_ref, wi_ref, ki_ref, k_ref, v_ref, o_ref, kib, kb, vb, key_ref, *, topk):
    i = pl.program_id(1)
    qb, n_keys = q_ref.shape[0], ki_ref.shape[0]
    group = q_ref.shape[1] // ATT_HD // ATT_KV_HEADS

    @pl.when(i == 0)
    def _():
        kib[...] = ki_ref[...].astype(BF16)
        kb[...] = k_ref[...].astype(BF16)
        vb[...] = v_ref[...].astype(BF16)

    score = jnp.zeros((qb, n_keys), F32)
    for h in range(IDX_HEADS):
        s = _bdot_nt(qi_ref[:, h * IDX_DIM:(h + 1) * IDX_DIM], kib[...]) * (IDX_DIM ** -0.5)
        score = score + jnp.maximum(s, 0.0) * wi_ref[:, h:h + 1]
    score = score * (IDX_HEADS ** -0.5)
    q_pos = i * qb + lax.broadcasted_iota(jnp.int32, (qb, n_keys), 0)
    causal = lax.broadcasted_iota(jnp.int32, (qb, n_keys), 1) <= q_pos
    key_ref[...] = _sortable_keys(jnp.where(causal, score, NEG_INF))

    def count_ge(t):
        return jnp.sum(jnp.where(key_ref[...] >= t, 1.0, 0.0), axis=-1, keepdims=True)

    thr = _kth_largest(count_ge, (qb, 1), topk)
    bias = jnp.where(causal, jnp.where(key_ref[...] >= thr, 0.0, NEG_INF), NEG_INF)
    bias = jnp.concatenate([bias] * group, axis=0)
    for n in range(ATT_KV_HEADS):
        heads = range(n * group, (n + 1) * group)
        qs = jnp.concatenate([q_ref[:, h * ATT_HD:(h + 1) * ATT_HD] for h in heads], axis=0)
        s = _bdot_nt(qs, kb[:, n * ATT_HD:(n + 1) * ATT_HD]) * (ATT_HD ** -0.5) + bias
        e = jnp.exp(s - jnp.max(s, axis=-1, keepdims=True))
        o = _bdot(e, vb[:, n * ATT_HD:(n + 1) * ATT_HD]) / jnp.sum(e, axis=-1, keepdims=True)
        for g, h in enumerate(heads):
            o_ref[:, h * ATT_HD:(h + 1) * ATT_HD] = o[g * qb:(g + 1) * qb, :].astype(BF16)


PAGES_PER_CHUNK = 8
GROUP_LANES = LANES // PAGES_PER_CHUNK


def _group_sum(x):
    sh = 1
    while sh < GROUP_LANES:
        x = x + pltpu.roll(x, sh, axis=1)
        sh *= 2
    return x


def _group_spread(x):
    sh = 1
    while sh < GROUP_LANES:
        x = jnp.maximum(x, pltpu.roll(x, LANES - sh, axis=1))
        sh *= 2
    return x


def _across_groups(x, op):
    sh = GROUP_LANES
    while sh < LANES:
        x = op(x, pltpu.roll(x, sh, axis=1))
        sh *= 2
    return x


def _dsa_step_kernel(pt_ref, qit_ref, wq_ref, wl_ref, kin_ref, kn_ref, vn_ref, ci_hbm, ck_hbm, cv_hbm, o_ref,
                     ibuf, kvbuf, isem, kvsem, qit_s, wq_s, key_ref, att_ref, *, topk):
    b = pl.program_id(0)
    nb = pl.num_programs(0)
    npc = PAGES_PER_CHUNK
    n_chunks = key_ref.shape[0]
    assert GROUP_LANES == IDX_HEADS and n_chunks % 2 == 0

    def page_copy(src, buf, sem, seq, chunk, slot, pg):
        return pltpu.make_async_copy(src.at[pt_ref[seq, chunk * npc + pg]], buf.at[slot, pg], sem.at[slot])

    def start(src, buf, sem, seq, chunk, slot):
        for pg in range(npc):
            page_copy(src, buf, sem, seq, chunk, slot, pg).start()

    def wait(src, buf, sem, slot):
        for pg in range(npc):
            page_copy(src, buf, sem, 0, 0, slot, pg).wait()

    @pl.when(b == 0)
    def _():
        start(ci_hbm, ibuf, isem, 0, 0, 0)

    lane = lax.broadcasted_iota(jnp.int32, (1, LANES), 1)
    lane_grp = lax.shift_right_logical(lane, int(math.log2(GROUP_LANES)))
    last_in_grp = (lane & (GROUP_LANES - 1)) == GROUP_LANES - 1
    for pg in range(npc):
        qit_s[pg] = jnp.where(lane_grp == pg, qit_ref[...], 0.0).astype(BF16)
        wq_s[pg] = jnp.where(lane_grp == pg, wq_ref[...], 0.0).astype(BF16)
    w_lane = wl_ref[...]

    def idx_score(raw):
        s = jnp.maximum(raw * (IDX_DIM ** -0.5), 0.0) * w_lane
        return _group_sum(s) * (IDX_HEADS ** -0.5)

    def idx_chunk(c, carry):
        slot = c & 1

        @pl.when(c + 1 < n_chunks)
        def _():
            start(ci_hbm, ibuf, isem, b, c + 1, 1 - slot)

        @pl.when(c + 1 == n_chunks)
        def _():
            start(ck_hbm, kvbuf, kvsem, b, 0, 0)

        wait(ci_hbm, ibuf, isem, slot)
        acc = jnp.zeros((PAGE_SIZE, LANES), F32)
        for pg in range(npc):
            acc = acc + jnp.dot(ibuf[slot, pg].astype(BF16), qit_s[pg], preferred_element_type=F32)
        key_ref[c] = _sortable_keys(jnp.where(last_in_grp, idx_score(acc), NEG_INF))
        return carry

    lax.fori_loop(0, n_chunks, idx_chunk, 0)
    rows8 = lambda r: jnp.broadcast_to(r, (8, r.shape[1]))
    new_raw = _bdot(rows8(kin_ref[...]), qit_ref[...])[0:1, :]
    key_new = _sortable_keys(jnp.where(lane == GROUP_LANES - 1, idx_score(new_raw), NEG_INF))

    def count_ge(t):
        past = jnp.sum(jnp.where(key_ref[...] >= t, 1.0, 0.0), axis=0)
        tot = jnp.sum(past, axis=0, keepdims=True) + jnp.where(key_new >= t, 1.0, 0.0)
        return jnp.sum(tot, axis=1, keepdims=True)

    thr = _kth_largest(count_ge, (1, 1), topk)
    new_sel = jnp.max(jnp.where(key_new >= thr, 1.0, 0.0), axis=1, keepdims=True) > 0.0

    def k_chunk(c, carry):
        slot = c & 1

        @pl.when(c + 1 < n_chunks)
        def _():
            start(ck_hbm, kvbuf, kvsem, b, c + 1, 1 - slot)

        @pl.when(c + 1 == n_chunks)
        def _():
            start(cv_hbm, kvbuf, kvsem, b, 0, 1 - slot)

        wait(ck_hbm, kvbuf, kvsem, slot)
        acc = jnp.zeros((PAGE_SIZE, LANES), F32)
        for pg in range(npc):
            acc = acc + jnp.dot(kvbuf[slot, pg].astype(BF16), wq_s[pg], preferred_element_type=F32)
        sel = _group_spread(jnp.where(key_ref[c] >= thr, 1.0, 0.0))
        att_ref[c] = jnp.where(sel > 0.0, acc * (ATT_HD ** -0.5), NEG_INF)
        return carry

    lax.fori_loop(0, n_chunks, k_chunk, 0)
    s_new = _bdot(rows8(kn_ref[...]), wq_ref[...])[0:1, :] * (ATT_HD ** -0.5)
    s_new = jnp.where(new_sel, s_new, NEG_INF)
    m = jnp.max(jnp.max(att_ref[...], axis=0), axis=0, keepdims=True)
    m = jnp.maximum(_across_groups(m, jnp.maximum), s_new)
    e_new = jnp.where(lane < GROUP_LANES, jnp.exp(s_new - m), 0.0)

    def v_chunk(c, carry):
        acc, lsum = carry
        slot = (c + n_chunks) & 1

        @pl.when(c + 1 < n_chunks)
        def _():
            start(cv_hbm, kvbuf, kvsem, b, c + 1, 1 - slot)

        @pl.when(jnp.logical_and(c + 1 == n_chunks, b + 1 < nb))
        def _():
            start(ci_hbm, ibuf, isem, b + 1, 0, 0)

        wait(cv_hbm, kvbuf, kvsem, slot)
        et = jnp.exp(att_ref[c] - m).T
        lsum = lsum + jnp.sum(et, axis=1, keepdims=True)
        for pg in range(npc):
            acc = acc + _bdot(et[pg * GROUP_LANES:(pg + 1) * GROUP_LANES, :], kvbuf[slot, pg])
        return acc, lsum

    n_kv = kn_ref.shape[1]
    acc, lsum = lax.fori_loop(0, n_chunks, v_chunk,
                              (jnp.zeros((GROUP_LANES, n_kv), F32), jnp.zeros((LANES, 1), F32)))
    row0 = lambda r, n: jnp.where(lax.broadcasted_iota(jnp.int32, (PAGE_SIZE, n), 0) == 0, r, 0.0)
    et_new = row0(e_new, LANES).T
    acc = acc + _bdot(et_new[0:GROUP_LANES, :], row0(vn_ref[...], n_kv))
    lsum = lsum + jnp.sum(et_new, axis=1, keepdims=True)
    l16 = lsum[0:GROUP_LANES, :]
    for pg in range(1, npc):
        l16 = l16 + lsum[pg * GROUP_LANES:(pg + 1) * GROUP_LANES, :]
    out = acc / l16
    kv_of_row = lax.broadcasted_iota(jnp.int32, (GROUP_LANES, ATT_HD), 0) // (GROUP_LANES // ATT_KV_HEADS)
    res = jnp.zeros((GROUP_LANES, ATT_HD), F32)
    for n in range(ATT_KV_HEADS):
        res = res + jnp.where(kv_of_row == n, out[:, n * ATT_HD:(n + 1) * ATT_HD], 0.0)
    o_ref[...] = res.astype(BF16)


def _dsa_step(z2d, cache_k, cache_v, cache_kidx, page_table, *, d_model):
    off = _odd_columns(d_model)
    bsz = z2d.shape[0]
    n_pages = page_table.shape[1]
    n_phys, page, kvh, hd = cache_k.shape
    n_kv = kvh * hd
    heads = d_model // hd
    assert page == PAGE_SIZE and heads == GROUP_LANES and n_pages % PAGES_PER_CHUNK == 0
    n_chunks = n_pages // PAGES_PER_CHUNK
    topk = min(INDEX_TOPK, (n_pages * page + 1) // 4)
    q = z2d[:, off["q"]:off["q"] + d_model].reshape(bsz, kvh, heads // kvh, hd)
    qi = z2d[:, off["qi"]:off["qi"] + IDX_HEADS * IDX_DIM].reshape(bsz, IDX_HEADS, IDX_DIM)
    tile = lambda a: jnp.tile(a, (1, 1, PAGES_PER_CHUNK))
    qit = tile(jnp.transpose(qi, (0, 2, 1)))
    wq = tile(jnp.einsum("bngd,nm->bndmg", q, jnp.eye(kvh, dtype=F32)).reshape(bsz, n_kv, heads))
    wl = tile(z2d[:, off["wi"]:off["wi"] + IDX_HEADS].reshape(bsz, 1, IDX_HEADS))
    kin = z2d[:, off["ki"]:off["ki"] + IDX_DIM].reshape(bsz, 1, IDX_DIM)
    kn = z2d[:, off["k"]:off["k"] + n_kv].reshape(bsz, 1, n_kv)
    vn = z2d[:, off["v"]:off["v"] + n_kv].reshape(bsz, 1, n_kv)
    per_seq = lambda a: pl.BlockSpec((None,) + a.shape[1:], lambda b, pt: (b, 0, 0))
    hbm = pl.BlockSpec(memory_space=pl.ANY)
    grid_spec = pltpu.PrefetchScalarGridSpec(
        num_scalar_prefetch=1,
        grid=(bsz,),
        in_specs=[per_seq(qit), per_seq(wq), per_seq(wl), per_seq(kin), per_seq(kn), per_seq(vn), hbm, hbm, hbm],
        out_specs=pl.BlockSpec((None, heads, hd), lambda b, pt: (b, 0, 0)),
        scratch_shapes=[
            pltpu.VMEM((2, PAGES_PER_CHUNK, page, IDX_DIM), F32),
            pltpu.VMEM((2, PAGES_PER_CHUNK, page, n_kv), F32),
            pltpu.SemaphoreType.DMA((2,)),
            pltpu.SemaphoreType.DMA((2,)),
            pltpu.VMEM((PAGES_PER_CHUNK, IDX_DIM, LANES), BF16),
            pltpu.VMEM((PAGES_PER_CHUNK, n_kv, LANES), BF16),
            pltpu.VMEM((n_chunks, page, LANES), jnp.int32),
            pltpu.VMEM((n_chunks, page, LANES), F32),
        ])
    o = pl.pallas_call(
        functools.partial(_dsa_step_kernel, topk=topk),
        grid_spec=grid_spec,
        out_shape=jax.ShapeDtypeStruct((bsz, heads, hd), BF16),
        compiler_params=_params(("arbitrary",), 40),
        name="dsa_step",
    )(page_table, qit, wq, wl, kin, kn, vn, cache_kidx, cache_k.reshape(n_phys, page, n_kv),
      cache_v.reshape(n_phys, page, n_kv))
    return o.reshape(bsz, d_model)


def _dsa_prompt(z2d, *, batch, seq, d_model, qb):
    off = _odd_columns(d_model)
    nq = seq // qb
    n_kv = ATT_KV_HEADS * ATT_HD
    topk = min(INDEX_TOPK, seq // 4)
    rows = lambda w, c: pl.BlockSpec((qb, w), lambda b, i: (b * nq + i, c // w))
    whole = lambda w, c: pl.BlockSpec((seq, w), lambda b, i: (b, c // w))
    return pl.pallas_call(
        functools.partial(_dsa_kernel, topk=topk),
        grid=(batch, nq),
        in_specs=[rows(d_model, off["q"]), rows(IDX_HEADS * IDX_DIM, off["qi"]), rows(LANES, off["wi"]),
                  whole(IDX_DIM, off["ki"]), whole(n_kv, off["k"]), whole(n_kv, off["v"])],
        out_specs=pl.BlockSpec((qb, d_model), lambda b, i: (b * nq + i, 0)),
        out_shape=jax.ShapeDtypeStruct((batch * seq, d_model), BF16),
        scratch_shapes=[pltpu.VMEM((seq, IDX_DIM), BF16), pltpu.VMEM((seq, n_kv), BF16), pltpu.VMEM((seq, n_kv), BF16),
                        pltpu.VMEM((qb, seq), jnp.int32)],
        compiler_params=_params(("parallel", "arbitrary"), 56),
        name="dsa_prompt",
    )(z2d, z2d, z2d, z2d, z2d, z2d)


def _trunk(x2d, mods, wts, cfg, even_fn, odd_fn):
    tm, tf, tpb = cfg["tm"], cfg["tf"], cfg["tpb"]
    ffn = functools.partial(_ffn, tm=tm, tf=tf, tiles_per_batch=tpb, vmem_mib=cfg["ffn_vmem"])
    x = x2d
    states = []
    depth = wts["norm_g"].shape[0]
    for layer in range(depth):
        g, mod = wts["norm_g"][layer], mods[layer]
        x = ffn(x, g[0], mod, 0, wts["w_ffn_in"], wts["w_ffn_out"], (layer, 0), wts["final_g"], final_norm=False)
        x, st = (even_fn if layer % 2 == 0 else odd_fn)(x, g[1], mod)
        states.append(st)
        x = ffn(x, g[2], mod, 2, wts["w_ffn_in"], wts["w_ffn_out"], (layer, 1), wts["final_g"],
                final_norm=layer == depth - 1)
    return x, states


def kernel(x_prompt, x_sample, c_prompt, c_sample, state_ret, state_s5_re, state_s5_im, cache_k, cache_v, cache_kidx,
           page_table, norm_g, w_ada, b_ada, w_ffn_in, w_ffn_out, w_in_even, w_out_even, s5_a_re, s5_a_im, s5_log_dt,
           s5_b_re, s5_b_im, s5_c_re, s5_c_im, s5_d, w_glu, b_glu, w_in_odd, w_out_odd, final_g):
    bp, lp, d = x_prompt.shape
    bs, ls, _ = x_sample.shape
    assert ls == 1
    depth = w_ada.shape[0]
    n_mod = N_SUBLAYERS * 3

    rows = -(-(bp + bs) // 8) * 8
    c_all = jnp.concatenate([c_prompt, c_sample, jnp.zeros((rows - bp - bs, d), F32)], axis=0)
    mod_all = _ada_mod(c_all, w_ada, b_ada)
    mods_p = [mod_all[l, :bp].reshape(bp, n_mod, 1, d) for l in range(depth)]
    mods_s = [jnp.transpose(mod_all[l, bp:bp + bs].reshape(bs, n_mod, d), (1, 0, 2))[None] for l in range(depth)]

    p = dict(w_in_even=w_in_even, w_out_even=w_out_even, s5_a_re=s5_a_re, s5_a_im=s5_a_im, s5_log_dt=s5_log_dt,
             s5_b_re=s5_b_re, s5_b_im=s5_b_im, s5_c_re=s5_c_re, s5_c_im=s5_c_im, s5_d=s5_d, w_glu=w_glu, b_glu=b_glu)
    sp = _s5_prepare(p)
    w_odd = _odd_weight(w_in_odd, d)
    off = _odd_columns(d)
    n_kv = ATT_KV_HEADS * ATT_HD
    wts = dict(norm_g=norm_g, w_ffn_in=w_ffn_in, w_ffn_out=w_ffn_out, final_g=final_g)
    odd_tn = 768

    def cache_parts(z, b, l):
        return (z[:, off["k"]:off["k"] + n_kv].reshape(b, l, ATT_KV_HEADS, ATT_HD),
                z[:, off["v"]:off["v"] + n_kv].reshape(b, l, ATT_KV_HEADS, ATT_HD),
                z[:, off["ki"]:off["ki"] + IDX_DIM].reshape(b, l, IDX_DIM))

    cfg_p = dict(tm=min(1024, lp), tf=256, ffn_vmem=60)
    cfg_p["tpb"] = lp // cfg_p["tm"]

    def even_p(x, g, mod):
        zero_s5 = jnp.zeros((bp,) + state_s5_re.shape[1:], F32)
        x, s_ret, sr, si = _even_layer(x, g, mod, p, jnp.zeros((bp,) + state_ret.shape[1:], F32), zero_s5, zero_s5,
                                       batch=bp, seq=lp, pos0=0, tm=cfg_p["tm"], tiles_per_batch=cfg_p["tpb"],
                                       rb=min(256, lp), chunk=min(RET_CHUNK, lp), s5_t=min(128, lp), sp=sp)
        return x, (s_ret, sr, si)

    def odd_p(x, g, mod):
        z = _proj(x, g, mod, 1, w_odd, tm=cfg_p["tm"], tn=odd_tn, tiles_per_batch=cfg_p["tpb"], vmem_mib=52)
        o = _dsa_prompt(z, batch=bp, seq=lp, d_model=d, qb=min(128, lp))
        x = _outproj([o], w_out_odd, x, mod, tm=cfg_p["tm"], tn=1024, tiles_per_batch=cfg_p["tpb"], vmem_mib=48)
        return x, cache_parts(z, bp, lp)

    y_p, (ev_p, od_p) = _trunk(x_prompt.reshape(bp * lp, d), mods_p, wts, cfg_p, even_p, odd_p)

    past_len = page_table.shape[1] * cache_k.shape[1]
    cfg_s = dict(tm=bs, tf=512, tpb=1, ffn_vmem=40)

    def even_s(x, g, mod):
        x, s_ret, sr, si = _even_layer(x, g, mod, p, state_ret, state_s5_re, state_s5_im, batch=bs, seq=1,
                                       pos0=past_len, tm=bs, tiles_per_batch=1, rb=1, chunk=1, s5_t=1, sp=sp)
        return x, (s_ret, sr, si)

    def odd_s(x, g, mod):
        z = _proj(x, g, mod, 1, w_odd, tm=bs, tn=odd_tn, tiles_per_batch=1, vmem_mib=40)
        o = _dsa_step(z, cache_k, cache_v, cache_kidx, page_table, d_model=d)
        x = _outproj([o], w_out_odd, x, mod, tm=bs, tn=1024, tiles_per_batch=1, vmem_mib=40)
        return x, cache_parts(z, bs, 1)

    y_s, (ev_s, od_s) = _trunk(x_sample.reshape(bs, d), mods_s, wts, cfg_s, even_s, odd_s)

    return (y_p.reshape(bp, lp, d), y_s.reshape(bs, 1, d), *ev_p, *od_p, *ev_s, *od_s)
```

```python
import functools
import math

import jax
import jax.numpy as jnp
from jax import lax
from jax.experimental import pallas as pl
from jax.experimental.pallas import tpu as pltpu

F32 = jnp.float32
BF16 = jnp.bfloat16

EPS = 1e-6
FFN_RES = 0.5
N_SUBLAYERS = 3
RET_HEADS = 4
RET_CHUNK = 128
ROPE_BASE = 10000.0
S5_GROUP_CH = 16
S5_STATE = 64
ATT_HD = 128
ATT_KV_HEADS = 4
IDX_HEADS = 16
IDX_DIM = 128
INDEX_TOPK = 256
PAGE_SIZE = 128

LANES = 128
MIB = 1024 * 1024
NEG_INF = float("-inf")


def _params(sem, vmem_mib):
    return pltpu.CompilerParams(dimension_semantics=sem, vmem_limit_bytes=vmem_mib * MIB)


def _bdot(a, b):
    return jnp.dot(a.astype(BF16), b.astype(BF16), preferred_element_type=F32)


def _bdot_nt(a, b):
    return lax.dot_general(a.astype(BF16), b.astype(BF16), (((1,), (1,)), ((), ())),
                           preferred_element_type=F32)


def _bdot_tn(a, b):
    return lax.dot_general(a.astype(BF16), b.astype(BF16), (((0,), (0,)), ((), ())),
                           preferred_element_type=F32)


def _silu(x):
    return x * jax.nn.sigmoid(x)


def _rms(x, g):
    ms = jnp.mean(x * x, axis=-1, keepdims=True)
    return x * lax.rsqrt(ms + EPS) * g


def _row_chunks(n_rows, chunk, fn):
    if n_rows <= chunk:
        fn(slice(0, n_rows))
        return
    assert n_rows % chunk == 0

    def body(i, c):
        fn(pl.ds(pl.multiple_of(i * chunk, chunk), chunk))
        return c

    lax.fori_loop(0, n_rows // chunk, body, 0)


ROW_CHUNK = 256


def _norm_mod_to(h_ref, x_ref, g_ref, sc_ref, sh_ref):
    n_rows = x_ref.shape[0]
    assert sc_ref.shape[0] == 1 or n_rows <= ROW_CHUNK

    def fn(rows):
        y = _rms(x_ref[rows, :], g_ref[...])
        h_ref[rows, :] = (y * (1.0 + sc_ref[...]) + sh_ref[...]).astype(BF16)

    _row_chunks(n_rows, ROW_CHUNK, fn)


def _ada_kernel(c_ref, w_ref, b_ref, o_ref):
    o_ref[...] = _bdot(_silu(c_ref[...]), w_ref[...]) + b_ref[...]


def _ada_mod(c_all, w_ada, b_ada):
    depth, d, n = w_ada.shape
    rows = c_all.shape[0]
    tn = 1024
    return pl.pallas_call(
        _ada_kernel,
        grid=(depth, n // tn),
        in_specs=[
            pl.BlockSpec((rows, d), lambda l, j: (0, 0)),
            pl.BlockSpec((None, d, tn), lambda l, j: (l, 0, j)),
            pl.BlockSpec((None, 1, tn), lambda l, j: (l, 0, j)),
        ],
        out_specs=pl.BlockSpec((None, rows, tn), lambda l, j: (l, 0, j)),
        out_shape=jax.ShapeDtypeStruct((depth, rows, n), F32),
        compiler_params=_params(("arbitrary", "arbitrary"), 40),
        name="ada_mod",
    )(c_all, w_ada, b_ada.reshape(depth, 1, n))


def _mod_specs(mod, sub, tiles_per_batch, which):
    _, _, r, d = mod.shape
    specs = []
    for k in which:
        idx = sub * 3 + k
        specs.append(pl.BlockSpec((None, None, r, d),
                                  lambda i, j, idx=idx: (i // tiles_per_batch, idx, 0, 0)))
    return specs


def _ffn_kernel(x_ref, g_ref, sh_ref, sc_ref, gt_ref, wa_ref, wb_ref, wo_ref, fg_ref, o_ref, h_ref,
                *, final_norm):
    j = pl.program_id(1)
    nj = pl.num_programs(1)

    @pl.when(j == 0)
    def _():
        _norm_mod_to(h_ref, x_ref, g_ref, sc_ref, sh_ref)

    h = h_ref[...]
    a = jnp.dot(h, wa_ref[...].astype(BF16), preferred_element_type=F32)
    b = jnp.dot(h, wb_ref[...].astype(BF16), preferred_element_type=F32)
    p = _bdot(_silu(a) * b, wo_ref[...])

    @pl.when(j == 0)
    def _():
        o_ref[...] = p

    @pl.when(j > 0)
    def _():
        o_ref[...] += p

    @pl.when(j == nj - 1)
    def _():
        def fn(rows):
            y = x_ref[rows, :] + (FFN_RES * gt_ref[...]) * o_ref[rows, :]
            if final_norm:
                y = _rms(y, fg_ref[...])
            o_ref[rows, :] = y

        _row_chunks(x_ref.shape[0], ROW_CHUNK, fn)


def _ffn(x2d, norm_g, mod, sub, w_in, w_out, wsel, final_g, *, tm, tf, tiles_per_batch, final_norm, vmem_mib):
    m, d = x2d.shape
    f = w_out.shape[2]
    nf = f // tf
    assert m % tm == 0 and f % tf == 0
    row = lambda i, j: (0, 0)
    l, k = wsel
    return pl.pallas_call(
        functools.partial(_ffn_kernel, final_norm=final_norm),
        grid=(m // tm, nf),
        in_specs=[
            pl.BlockSpec((tm, d), lambda i, j: (i, 0)),
            pl.BlockSpec((1, d), row),
            *_mod_specs(mod, sub, tiles_per_batch, (0, 1, 2)),
            pl.BlockSpec((None, None, d, tf), lambda i, j: (l, k, 0, j)),
            pl.BlockSpec((None, None, d, tf), lambda i, j: (l, k, 0, nf + j)),
            pl.BlockSpec((None, None, tf, d), lambda i, j: (l, k, j, 0)),
            pl.BlockSpec((1, d), row),
        ],
        out_specs=pl.BlockSpec((tm, d), lambda i, j: (i, 0)),
        out_shape=jax.ShapeDtypeStruct((m, d), F32),
        scratch_shapes=[pltpu.VMEM((tm, d), BF16)],
        compiler_params=_params(("parallel", "arbitrary"), vmem_mib),
        name="ffn",
    )(x2d, norm_g.reshape(1, d), mod, mod, mod, w_in, w_in, w_out, final_g.reshape(1, d))


def _proj_kernel(x_ref, g_ref, sh_ref, sc_ref, w_ref, o_ref, h_ref):
    @pl.when(pl.program_id(1) == 0)
    def _():
        _norm_mod_to(h_ref, x_ref, g_ref, sc_ref, sh_ref)

    o_ref[...] = jnp.dot(h_ref[...], w_ref[...].astype(BF16), preferred_element_type=F32)


def _proj(x2d, norm_g, mod, sub, w, *, tm, tn, tiles_per_batch, vmem_mib):
    m, d = x2d.shape
    n = w.shape[1]
    assert m % tm == 0 and n % tn == 0
    return pl.pallas_call(
        _proj_kernel,
        grid=(m // tm, n // tn),
        in_specs=[
            pl.BlockSpec((tm, d), lambda i, j: (i, 0)),
            pl.BlockSpec((1, d), lambda i, j: (0, 0)),
            *_mod_specs(mod, sub, tiles_per_batch, (0, 1)),
            pl.BlockSpec((d, tn), lambda i, j: (0, j)),
        ],
        out_specs=pl.BlockSpec((tm, tn), lambda i, j: (i, j)),
        out_shape=jax.ShapeDtypeStruct((m, n), F32),
        scratch_shapes=[pltpu.VMEM((tm, d), BF16)],
        compiler_params=_params(("parallel", "arbitrary"), vmem_mib),
        name="mixer_in_proj",
    )(x2d, norm_g.reshape(1, d), mod, mod, w)


def _outproj_kernel(*refs, n_parts):
    a_refs = refs[:n_parts]
    w_ref, x_ref, gt_ref, o_ref = refs[n_parts:]
    acc = None
    k0 = 0
    for a_ref in a_refs:
        kk = a_ref.shape[1]
        p = jnp.dot(a_ref[...], w_ref[k0:k0 + kk, :].astype(BF16), preferred_element_type=F32)
        acc = p if acc is None else acc + p
        k0 += kk
    o_ref[...] = x_ref[...] + gt_ref[...] * acc


def _outproj(a_parts, w, x2d, mod, *, tm, tn, tiles_per_batch, vmem_mib):
    m, d = x2d.shape
    k = w.shape[0]
    assert sum(kw for _, kw, _ in a_parts) == k and m % tm == 0 and d % tn == 0
    _, _, r, _ = mod.shape
    return pl.pallas_call(
        functools.partial(_outproj_kernel, n_parts=len(a_parts)),
        grid=(m // tm, d // tn),
        in_specs=[
            *[pl.BlockSpec((tm, kw), imap or (lambda i, j: (i, 0))) for _, kw, imap in a_parts],
            pl.BlockSpec((k, tn), lambda i, j: (0, j)),
            pl.BlockSpec((tm, tn), lambda i, j: (i, j)),
            pl.BlockSpec((None, None, r, tn), lambda i, j: (i // tiles_per_batch, 5, 0, j)),
        ],
        out_specs=pl.BlockSpec((tm, tn), lambda i, j: (i, j)),
        out_shape=jax.ShapeDtypeStruct((m, d), F32),
        compiler_params=_params(("parallel", "arbitrary"), vmem_mib),
        name="mixer_out_proj",
    )(*[a for a, _, _ in a_parts], w, x2d, mod)


def _rot(x1, x2, cos, sin):
    return jnp.concatenate([x1 * cos - x2 * sin, x1 * sin + x2 * cos], axis=-1)


def _head_norm(o):
    mu = jnp.mean(o, axis=-1, keepdims=True)
    var = jnp.mean(jnp.square(o - mu), axis=-1, keepdims=True)
    return (o - mu) * lax.rsqrt(var + 1e-5)


def _ret_log_g(h):
    return math.log1p(-(2.0 ** (-5.0 - h)))


def _ret_kernel(q_ref, k_ref, v_ref, g_ref, cos_ref, sin_ref, s0_ref, o_ref, s_ref, *, chunk):
    c = pl.program_id(1)
    rb = q_ref.shape[0]
    dk = q_ref.shape[1] // RET_HEADS
    half = dk // 2

    @pl.when(c == 0)
    def _():
        s_ref[...] = s0_ref[...]

    ri = lax.broadcasted_iota(jnp.int32, (chunk, chunk), 0).astype(F32)
    ci = lax.broadcasted_iota(jnp.int32, (chunk, chunk), 1).astype(F32)
    diff = ri - ci
    rowf = lax.broadcasted_iota(jnp.int32, (chunk, dk), 0).astype(F32)
    for h in range(RET_HEADS):
        lg = _ret_log_g(h)
        dmat = jnp.where(diff >= 0, jnp.exp(jnp.maximum(diff, 0.0) * lg), 0.0)
        q_dec = jnp.exp((rowf + 1.0) * lg)
        k_dec = jnp.exp((chunk - 1.0 - rowf) * lg)
        c_dec = math.exp(chunk * lg)
        c0 = h * dk
        for ck in range(rb // chunk):
            rows = slice(ck * chunk, (ck + 1) * chunk)
            cos = cos_ref[rows, :]
            sin = sin_ref[rows, :]
            qr = _rot(q_ref[rows, c0:c0 + half], q_ref[rows, c0 + half:c0 + dk], cos, sin)
            kr = _rot(k_ref[rows, c0:c0 + half], k_ref[rows, c0 + half:c0 + dk], cos, sin) * (dk ** -0.5)
            vb = v_ref[rows, c0:c0 + dk].astype(BF16)
            qb = qr.astype(BF16)
            a = _bdot_nt(qb, kr) * dmat
            s = s_ref[h]
            o = _bdot(a, vb) + _bdot(qb, s) * q_dec
            s_ref[h] = s * c_dec + _bdot_tn(kr * k_dec, vb)
            o_ref[rows, c0:c0 + dk] = (_head_norm(o) * _silu(g_ref[rows, c0:c0 + dk])).astype(BF16)


def _retention(z2d, col0, cos, sin, state0, *, batch, seq, rb, chunk):
    _, h, dk, dv = state0.shape
    w = h * dk
    nb = seq // rb
    assert seq % rb == 0 and rb % chunk == 0 and col0 % w == 0
    cb = col0 // w
    zspec = lambda k: pl.BlockSpec((rb, w), lambda b, c, k=k: (b * nb + c, cb + k))
    tab = pl.BlockSpec((rb, dk // 2), lambda b, c: (c, 0))
    st = pl.BlockSpec((None, h, dk, dv), lambda b, c: (b, 0, 0, 0))
    return pl.pallas_call(
        functools.partial(_ret_kernel, chunk=chunk),
        grid=(batch, nb),
        in_specs=[zspec(0), zspec(1), zspec(2), zspec(3), tab, tab, st],
        out_specs=[pl.BlockSpec((rb, w), lambda b, c: (b * nb + c, 0)), st],
        out_shape=[jax.ShapeDtypeStruct((batch * seq, w), BF16),
                   jax.ShapeDtypeStruct(state0.shape, F32)],
        compiler_params=_params(("parallel", "arbitrary"), 40),
        name="retention",
    )(z2d, z2d, z2d, z2d, cos, sin, state0)


def _rope_tables(pos, half):
    freqs = ROPE_BASE ** (-jnp.arange(half, dtype=F32) / half)
    ang = pos.astype(F32)[:, None] * freqs[None, :]
    return jnp.cos(ang), jnp.sin(ang)


def _r16(x):
    return x.astype(BF16).astype(F32)


def _ret_step_kernel(q_ref, krow_ref, kcol_ref, v_ref, g_ref, cos_ref, sin_ref, cosc_ref, sinc_ref, s0_ref,
                     o_ref, s_ref):
    dk = q_ref.shape[1] // RET_HEADS
    half = dk // 2
    cos, sin = cos_ref[...], sin_ref[...]
    cosc, sinc = cosc_ref[...], sinc_ref[...]
    for h in range(RET_HEADS):
        dec = math.exp(_ret_log_g(h))
        c0 = h * dk
        qr = _r16(_rot(q_ref[:, c0:c0 + half], q_ref[:, c0 + half:c0 + dk], cos, sin))
        kr = _r16(_rot(krow_ref[:, c0:c0 + half], krow_ref[:, c0 + half:c0 + dk], cos, sin) * (dk ** -0.5))
        k1, k2 = kcol_ref[h, 0:half, :], kcol_ref[h, half:dk, :]
        kc = _r16(jnp.concatenate([k1 * cosc - k2 * sinc, k1 * sinc + k2 * cosc], axis=0) * (dk ** -0.5))
        v = _r16(v_ref[:, c0:c0 + dk])
        a = jnp.sum(qr * kr, axis=-1, keepdims=True)
        s = s0_ref[h]
        o = _r16(a) * v + _bdot(qr, s) * dec
        s_ref[h] = s * dec + kc * v
        o_ref[:, c0:c0 + dk] = (_head_norm(o) * _silu(g_ref[:, c0:c0 + dk])).astype(BF16)


def _retention_step(z2d, cos, sin, state0):
    b, h, dk, dv = state0.shape
    w = h * dk
    z3 = z2d.reshape(b, 1, z2d.shape[1])
    kcol = z2d[:, w:2 * w].reshape(b, h, dk, 1)
    half = dk // 2
    zspec = lambda k: pl.BlockSpec((None, 1, w), lambda i, k=k: (i, 0, k))
    row = pl.BlockSpec((1, half), lambda i: (0, 0))
    col = pl.BlockSpec((half, 1), lambda i: (0, 0))
    st = pl.BlockSpec((None, h, dk, dv), lambda i: (i, 0, 0, 0))
    og, s = pl.pallas_call(
        _ret_step_kernel,
        grid=(b,),
        in_specs=[zspec(0), zspec(1), pl.BlockSpec((None, h, dk, 1), lambda i: (i, 0, 0, 0)), zspec(2), zspec(3),
                  row, row, col, col, st],
        out_specs=[pl.BlockSpec((None, 1, w), lambda i: (i, 0, 0)), st],
        out_shape=[jax.ShapeDtypeStruct((b, 1, w), BF16), jax.ShapeDtypeStruct(state0.shape, F32)],
        compiler_params=_params(("parallel",), 32),
        name="retention_step",
    )(z3, z3, kcol, z3, z3, cos, sin, cos.reshape(half, 1), sin.reshape(half, 1), state0)
    return og.reshape(b, w), s


S5_BLOCKS = 8


def _s5_disc(ar, ai, dt):
    mag = jnp.exp(dt * ar)
    abr = mag * jnp.cos(dt * ai)
    abi = mag * jnp.sin(dt * ai)
    den = ar * ar + ai * ai
    xr = abr - 1.0
    return abr, abi, (xr * ar + abi * ai) / den, (abi * ar - xr * ai) / den


def _s5_param_kernel(ar_ref, ai_ref, ldt_ref, arr_ref, air_ref, br_ref, bi_ref, abr_ref, abi_ref, bbr_ref, bbi_ref):
    dt = jnp.exp(ldt_ref[...])
    abr_ref[...], abi_ref[...], _, _ = _s5_disc(ar_ref[...], ai_ref[...], dt)
    _, _, fr, fi = _s5_disc(arr_ref[...], air_ref[...], dt)
    br, bi = br_ref[...], bi_ref[...]
    bbr_ref[...] = fr * br - fi * bi
    bbi_ref[...] = fr * bi + fi * br


def _s5_prepare(p):
    g, st = p["s5_a_re"].shape
    ch = p["s5_d"].shape[1]
    rep = lambda a: jnp.repeat(a, ch, axis=1)
    abr, abi, bbr, bbi = pl.pallas_call(
        _s5_param_kernel,
        out_shape=[jax.ShapeDtypeStruct((g, st), F32)] * 2 + [jax.ShapeDtypeStruct((g, st * ch), F32)] * 2,
        name="s5_discretise",
    )(p["s5_a_re"], p["s5_a_im"], p["s5_log_dt"].reshape(g, 1), rep(p["s5_a_re"]), rep(p["s5_a_im"]),
      p["s5_b_re"].reshape(g, st * ch), p["s5_b_im"].reshape(g, st * ch))
    nb = S5_BLOCKS
    gl = g // nb
    eye = jnp.eye(gl, dtype=F32)
    w_in = lambda bb: jnp.einsum("bgpc,gh->bgchp", bb.reshape(nb, gl, st, ch), eye).reshape(nb, gl * ch, gl * st)
    w_out = lambda c: jnp.einsum("bgcp,gh->bgphc", c.reshape(nb, gl, ch, st), eye).reshape(nb, gl * st, gl * ch)
    return dict(
        abr=abr.reshape(1, g * st), abi=abi.reshape(1, g * st),
        wbr=w_in(bbr).astype(BF16), wbi=w_in(bbi).astype(BF16),
        wcr=w_out(p["s5_c_re"]).astype(BF16), wci=w_out(p["s5_c_im"]).astype(BF16),
        d=p["s5_d"].reshape(1, g * ch), w_glu=p["w_glu"], b_glu=p["b_glu"].reshape(1, -1))


S5_SCAN_LANES = 1024


def _s5_kernel(u_ref, x0r_ref, x0i_ref, abr_ref, abi_ref, wbr_ref, wbi_ref, wcr_ref, wci_ref, d_ref, wg_ref,
               bg_ref, y_ref, xr_ref, xi_ref, bur, bui, yb, wgb, *, ns, t):
    nblk = wbr_ref.shape[0]
    uw, sw = wbr_ref.shape[1], wbr_ref.shape[2]
    n_tiles = nblk * sw // LANES
    sub = 8

    @pl.when(pl.program_id(0) == 0)
    def _():
        xr_ref[...] = x0r_ref[...]
        xi_ref[...] = x0i_ref[...]
        wgb[...] = wg_ref[...].astype(BF16)

    tiles_per_blk = sw // LANES
    lane_tile = lambda j: slice(j * LANES, (j + 1) * LANES)
    for b in range(nblk):
        ub = u_ref[:, b * uw:(b + 1) * uw].astype(BF16)
        pr = jnp.dot(ub, wbr_ref[b], preferred_element_type=F32)
        pi = jnp.dot(ub, wbi_ref[b], preferred_element_type=F32)
        for k in range(tiles_per_blk):
            bur[b * tiles_per_blk + k] = pr[:, lane_tile(k)]
            bui[b * tiles_per_blk + k] = pi[:, lane_tile(k)]

    def affine(ar, ai, xr, xi, vr, vi):
        return ar * xr - ai * xi + vr, ar * xi + ai * xr + vi

    per_pass = S5_SCAN_LANES // LANES
    for j0 in range(0, n_tiles, per_pass):
        tiles = list(range(j0, min(j0 + per_pass, n_tiles)))
        if t == 1:
            for j in tiles:
                ar = jnp.broadcast_to(abr_ref[:, lane_tile(j)], (ns, LANES))
                ai = jnp.broadcast_to(abi_ref[:, lane_tile(j)], (ns, LANES))
                nr, ni = affine(ar, ai, xr_ref[:, lane_tile(j)], xi_ref[:, lane_tile(j)], bur[j], bui[j])
                bur[j], bui[j] = nr, ni
                xr_ref[:, lane_tile(j)], xi_ref[:, lane_tile(j)] = nr, ni
            continue
        assert 2 * ns == sub and t % 2 == 0
        ar = [jnp.broadcast_to(abr_ref[:, lane_tile(j)], (sub, LANES)) for j in tiles]
        ai = [jnp.broadcast_to(abi_ref[:, lane_tile(j)], (sub, LANES)) for j in tiles]
        top = lax.broadcasted_iota(jnp.int32, (sub, LANES), 0) < ns
        swap = lambda x: pltpu.roll(x, ns, axis=0)

        def step(i, carry, tiles=tiles, ar=ar, ai=ai):
            rows = pl.ds(pl.multiple_of(i * sub, sub), sub)
            out = []
            for k, j in enumerate(tiles):
                pr_, pi_ = carry[k]
                vr, vi = bur[j, rows, :], bui[j, rows, :]
                er, ei = affine(ar[k], ai[k], swap(pr_), swap(pi_), vr, vi)
                orr, oi = affine(ar[k], ai[k], swap(er), swap(ei), vr, vi)
                nr, ni = jnp.where(top, er, orr), jnp.where(top, ei, oi)
                bur[j, rows, :] = nr
                bui[j, rows, :] = ni
                out.append((nr, ni))
            return tuple(out)

        twice = lambda x: jnp.concatenate([x, x], axis=0)
        fin = lax.fori_loop(0, t // 2, step,
                            tuple((twice(xr_ref[:, lane_tile(j)]), twice(xi_ref[:, lane_tile(j)])) for j in tiles))
        for k, j in enumerate(tiles):
            xr_ref[:, lane_tile(j)] = fin[k][0][ns:, :]
            xi_ref[:, lane_tile(j)] = fin[k][1][ns:, :]

    blk_states = lambda ref, b: jnp.concatenate(
        [ref[b * tiles_per_blk + k] for k in range(tiles_per_blk)], axis=1).astype(BF16)
    for b in range(nblk):
        yb[:, b * uw:(b + 1) * uw] = (jnp.dot(blk_states(bur, b), wcr_ref[b], preferred_element_type=F32)
                                      - jnp.dot(blk_states(bui, b), wci_ref[b], preferred_element_type=F32))
    gl = jax.nn.gelu(yb[...] + d_ref[...] * u_ref[...])
    gate = jax.nn.sigmoid(jnp.dot(gl.astype(BF16), wgb[...], preferred_element_type=F32) + bg_ref[...])
    y_ref[...] = (gl * gate).astype(BF16)


def _s5(u2d, col_blk, sp, x0r, x0i, *, batch, t):
    w = sp["d"].shape[1]
    n_state = sp["abr"].shape[1]
    rows = batch * t
    assert u2d.shape[0] % rows == 0
    full = lambda a: pl.BlockSpec(a.shape, lambda c, nd=a.ndim: (0,) * nd)
    st_spec = pl.BlockSpec((batch, n_state), lambda c: (0, 0))
    consts = [sp["abr"], sp["abi"], sp["wbr"], sp["wbi"], sp["wcr"], sp["wci"], sp["d"], sp["w_glu"], sp["b_glu"]]
    return pl.pallas_call(
        functools.partial(_s5_kernel, ns=batch, t=t),
        grid=(u2d.shape[0] // rows,),
        in_specs=[pl.BlockSpec((rows, w), lambda c: (c, col_blk)), st_spec, st_spec] + [full(a) for a in consts],
        out_specs=[pl.BlockSpec((rows, w), lambda c: (c, 0)), st_spec, st_spec],
        out_shape=[jax.ShapeDtypeStruct((u2d.shape[0], w), BF16), jax.ShapeDtypeStruct((batch, n_state), F32),
                   jax.ShapeDtypeStruct((batch, n_state), F32)],
        scratch_shapes=[pltpu.VMEM((n_state // LANES, rows, LANES), F32), pltpu.VMEM((n_state // LANES, rows, LANES), F32),
                        pltpu.VMEM((rows, w), F32), pltpu.VMEM(sp["w_glu"].shape, BF16)],
        compiler_params=_params(("arbitrary",), 48),
        name="s5",
    )(u2d, x0r.reshape(batch, n_state), x0i.reshape(batch, n_state), *consts)


def _even_layer(x2d, norm_g, mod, p, ret0, s5r0, s5i0, *, batch, seq, pos0, tm, tiles_per_batch, rb, chunk, s5_t,
                sp=None):
    _, h, dk, _ = ret0.shape
    rw = h * dk
    z = _proj(x2d, norm_g, mod, 1, p["w_in_even"], tm=tm, tn=1024, tiles_per_batch=tiles_per_batch, vmem_mib=52)
    cos, sin = _rope_tables(pos0 + jnp.arange(seq), dk // 2)
    if seq > 1:
        og, s_ret = _retention(z, 0, cos, sin, ret0, batch=batch, seq=seq, rb=rb, chunk=chunk)
    else:
        og, s_ret = _retention_step(z, cos, sin, ret0)
    sp = _s5_prepare(p) if sp is None else sp
    sw = sp["d"].shape[1]
    if seq > 1:
        u2d = jnp.transpose(z.reshape(batch, seq, -1)[:, :, 4 * rw:4 * rw + sw], (1, 0, 2)).reshape(seq * batch, sw)
        y, sr, si = _s5(u2d, 0, sp, s5r0, s5i0, batch=batch, t=s5_t)
        y_part = (y.reshape(seq, batch * sw), sw, lambda i, j: (i % tiles_per_batch, i // tiles_per_batch))
    else:
        y, sr, si = _s5(z, (4 * rw) // sw, sp, s5r0, s5i0, batch=batch, t=1)
        y_part = (y, sw, None)
    x_new = _outproj([(og, rw, None), y_part], p["w_out_even"], x2d, mod, tm=tm, tn=1024,
                     tiles_per_batch=tiles_per_batch, vmem_mib=48)
    return x_new, s_ret, sr.reshape(s5r0.shape), si.reshape(s5i0.shape)


INT32_MIN = -2 ** 31


def _sortable_keys(score):
    score = jnp.where(score == 0.0, 0.0, score)
    bits = lax.bitcast_convert_type(score, jnp.int32)
    return bits ^ (lax.shift_right_arithmetic(bits, 31) & 0x7FFFFFFF)


def _kth_largest(count_ge, shape, k):
    kf = jnp.float32(k)
    prefix = jnp.where(count_ge(jnp.zeros(shape, jnp.int32)) >= kf, 0, INT32_MIN).astype(jnp.int32)

    def body(i, prefix):
        cand = prefix | lax.shift_left(jnp.int32(1), 30 - i)
        return jnp.where(count_ge(cand) >= kf, cand, prefix)

    return lax.fori_loop(0, 31, body, prefix)


def _odd_columns(d_model):
    n_q = d_model
    n_kv = ATT_KV_HEADS * ATT_HD
    n_qi = IDX_HEADS * IDX_DIM
    off = dict(q=0, qi=n_q, k=n_q + n_qi, v=n_q + n_qi + n_kv, ki=n_q + n_qi + 2 * n_kv)
    off["wi"] = off["ki"] + IDX_DIM
    off["end"] = off["wi"] + LANES
    return off


def _odd_weight(w_in_odd, d_model):
    n_kv = ATT_KV_HEADS * ATT_HD
    n_qi = IDX_HEADS * IDX_DIM
    cuts = [d_model, d_model + n_kv, d_model + 2 * n_kv, d_model + 2 * n_kv + n_qi, d_model + 2 * n_kv + n_qi + IDX_DIM]
    q, k, v, qi, ki, wi = jnp.split(w_in_odd, cuts, axis=1)
    pad = jnp.zeros((w_in_odd.shape[0], LANES - wi.shape[1]), w_in_odd.dtype)
    return jnp.concatenate([q, qi, k, v, ki, wi, pad], axis=1)


def _dsa_kernel(q_ref, qi_ref, wi_ref, ki_ref, k_ref, v_ref, o_ref, kib, kb, vb, key_ref, *, topk):
    i = pl.program_id(1)
    qb, n_keys = q_ref.shape[0], ki_ref.shape[0]
    group = q_ref.shape[1] // ATT_HD // ATT_KV_HEADS

    @pl.when(i == 0)
    def _():
        kib[...] = ki_ref[...].astype(BF16)
        kb[...] = k_ref[...].astype(BF16)
        vb[...] = v_ref[...].astype(BF16)

    score = jnp.zeros((qb, n_keys), F32)
    for h in range(IDX_HEADS):
        s = _bdot_nt(qi_ref[:, h * IDX_DIM:(h + 1) * IDX_DIM], kib[...]) * (IDX_DIM ** -0.5)
        score = score + jnp.maximum(s, 0.0) * wi_ref[:, h:h + 1]
    score = score * (IDX_HEADS ** -0.5)
    q_pos = i * qb + lax.broadcasted_iota(jnp.int32, (qb, n_keys), 0)
    causal = lax.broadcasted_iota(jnp.int32, (qb, n_keys), 1) <= q_pos
    key_ref[...] = _sortable_keys(jnp.where(causal, score, NEG_INF))

    def count_ge(t):
        return jnp.sum(jnp.where(key_ref[...] >= t, 1.0, 0.0), axis=-1, keepdims=True)

    thr = _kth_largest(count_ge, (qb, 1), topk)
    bias = jnp.where(causal, jnp.where(key_ref[...] >= thr, 0.0, NEG_INF), NEG_INF)
    bias = jnp.concatenate([bias] * group, axis=0)
    for n in range(ATT_KV_HEADS):
        heads = range(n * group, (n + 1) * group)
        qs = jnp.concatenate([q_ref[:, h * ATT_HD:(h + 1) * ATT_HD] for h in heads], axis=0)
        s = _bdot_nt(qs, kb[:, n * ATT_HD:(n + 1) * ATT_HD]) * (ATT_HD ** -0.5) + bias
        e = jnp.exp(s - jnp.max(s, axis=-1, keepdims=True))
        o = _bdot(e, vb[:, n * ATT_HD:(n + 1) * ATT_HD]) / jnp.sum(e, axis=-1, keepdims=True)
        for g, h in enumerate(heads):
            o_ref[:, h * ATT_HD:(h + 1) * ATT_HD] = o[g * qb:(g + 1) * qb, :].astype(BF16)


PAGES_PER_CHUNK = 8
GROUP_LANES = LANES // PAGES_PER_CHUNK


def _group_sum(x):
    sh = 1
    while sh < GROUP_LANES:
        x = x + pltpu.roll(x, sh, axis=1)
        sh *= 2
    return x


def _group_spread(x):
    sh = 1
    while sh < GROUP_LANES:
        x = jnp.maximum(x, pltpu.roll(x, LANES - sh, axis=1))
        sh *= 2
    return x


def _across_groups(x, op):
    sh = GROUP_LANES
    while sh < LANES:
        x = op(x, pltpu.roll(x, sh, axis=1))
        sh *= 2
    return x


def _dsa_step_kernel(pt_ref, qit_ref, wq_ref, wl_ref, kin_ref, kn_ref, vn_ref, ex_ref, ci_hbm, ck_hbm, cv_hbm, o_ref,
                     ibuf, kvbuf, isem, kvsem, qit_s, wq_s, key_ref, att_ref, *, topk):
    b = pl.program_id(0)
    nb = pl.num_programs(0)
    npc = PAGES_PER_CHUNK
    n_chunks = key_ref.shape[0]
    assert GROUP_LANES == IDX_HEADS and n_chunks % 2 == 0

    def page_copy(src, buf, sem, seq, chunk, slot, pg):
        return pltpu.make_async_copy(src.at[pt_ref[seq, chunk * npc + pg]], buf.at[slot, pg], sem.at[slot])

    def start(src, buf, sem, seq, chunk, slot):
        for pg in range(npc):
            page_copy(src, buf, sem, seq, chunk, slot, pg).start()

    def wait(src, buf, sem, slot):
        for pg in range(npc):
            page_copy(src, buf, sem, 0, 0, slot, pg).wait()

    @pl.when(b == 0)
    def _():
        start(ci_hbm, ibuf, isem, 0, 0, 0)

    lane = lax.broadcasted_iota(jnp.int32, (1, LANES), 1)
    lane_grp = lax.shift_right_logical(lane, int(math.log2(GROUP_LANES)))
    last_in_grp = (lane & (GROUP_LANES - 1)) == GROUP_LANES - 1
    for pg in range(npc):
        qit_s[pg] = jnp.where(lane_grp == pg, qit_ref[...], 0.0).astype(BF16)
        wq_s[pg] = jnp.where(lane_grp == pg, wq_ref[...], 0.0).astype(BF16)
    w_lane = wl_ref[...]
    kv_rows = kvbuf.shape[2]
    q_per_kv = GROUP_LANES // ATT_KV_HEADS
    kv_of_lane = (lane & (GROUP_LANES - 1)) // q_per_kv
    pairs = lambda n: (lax.broadcasted_iota(jnp.int32, (n, LANES), 0) & (ATT_KV_HEADS - 1)) == kv_of_lane
    first_rows = lambda n: lax.broadcasted_iota(jnp.int32, (n, LANES), 0) < ATT_KV_HEADS

    def idx_score(raw):
        s = jnp.maximum(raw * (IDX_DIM ** -0.5), 0.0) * w_lane
        return _group_sum(s) * (IDX_HEADS ** -0.5)

    def idx_chunk(c, carry):
        slot = c & 1

        @pl.when(c + 1 < n_chunks)
        def _():
            start(ci_hbm, ibuf, isem, b, c + 1, 1 - slot)

        @pl.when(c + 1 == n_chunks)
        def _():
            start(ck_hbm, kvbuf, kvsem, b, 0, 0)

        wait(ci_hbm, ibuf, isem, slot)
        acc = jnp.zeros((PAGE_SIZE, LANES), F32)
        for pg in range(npc):
            acc = acc + jnp.dot(ibuf[slot, pg].astype(BF16), qit_s[pg], preferred_element_type=F32)
        key_ref[c] = _sortable_keys(jnp.where(last_in_grp, idx_score(acc), NEG_INF))
        return carry

    lax.fori_loop(0, n_chunks, idx_chunk, 0)
    rows8 = lambda r: jnp.broadcast_to(r, (8, r.shape[1]))
    new_raw = _bdot(rows8(kin_ref[...]), qit_ref[...])[0:1, :]
    key_new = _sortable_keys(jnp.where(lane == GROUP_LANES - 1, idx_score(new_raw), NEG_INF))

    def count_ge(t):
        past = jnp.sum(jnp.where(key_ref[...] >= t, 1.0, 0.0), axis=0)
        tot = jnp.sum(past, axis=0, keepdims=True) + jnp.where(key_new >= t, 1.0, 0.0)
        return jnp.sum(tot, axis=1, keepdims=True)

    thr = _kth_largest(count_ge, (1, 1), topk)
    new_sel = jnp.max(jnp.where(key_new >= thr, 1.0, 0.0), axis=1, keepdims=True) > 0.0

    def k_chunk(c, carry):
        slot = c & 1

        @pl.when(c + 1 < n_chunks)
        def _():
            start(ck_hbm, kvbuf, kvsem, b, c + 1, 1 - slot)

        @pl.when(c + 1 == n_chunks)
        def _():
            start(cv_hbm, kvbuf, kvsem, b, 0, 1 - slot)

        wait(ck_hbm, kvbuf, kvsem, slot)
        acc = jnp.zeros((kv_rows, LANES), F32)
        for pg in range(npc):
            acc = acc + jnp.dot(kvbuf[slot, pg].astype(BF16), wq_s[pg], preferred_element_type=F32)
        sel = _group_spread(jnp.where(key_ref[c] >= thr, 1.0, 0.0))
        sel_rows = jnp.dot(ex_ref[...], sel.astype(BF16), preferred_element_type=F32)
        att_ref[c] = jnp.where(pairs(kv_rows), jnp.where(sel_rows > 0.0, acc * (ATT_HD ** -0.5), NEG_INF), NEG_INF)
        return carry

    lax.fori_loop(0, n_chunks, k_chunk, 0)
    new_pairs = jnp.where(first_rows(8), jnp.where(pairs(8), 1.0, 0.0), 0.0)
    s_new = jnp.sum(new_pairs * _bdot(kn_ref[...], wq_ref[...]), axis=0, keepdims=True)
    s_new = jnp.where(new_sel, s_new * (ATT_HD ** -0.5), NEG_INF)
    m = jnp.max(jnp.max(att_ref[...], axis=0), axis=0, keepdims=True)
    m = jnp.maximum(_across_groups(m, jnp.maximum), s_new)
    e_new = jnp.where(lane < GROUP_LANES, jnp.exp(s_new - m), 0.0)

    def v_chunk(c, carry):
        acc, lsum = carry
        slot = (c + n_chunks) & 1

        @pl.when(c + 1 < n_chunks)
        def _():
            start(cv_hbm, kvbuf, kvsem, b, c + 1, 1 - slot)

        @pl.when(jnp.logical_and(c + 1 == n_chunks, b + 1 < nb))
        def _():
            start(ci_hbm, ibuf, isem, b + 1, 0, 0)

        wait(cv_hbm, kvbuf, kvsem, slot)
        et = jnp.exp(att_ref[c] - m).T
        lsum = lsum + jnp.sum(et, axis=1, keepdims=True)
        for pg in range(npc):
            acc = acc + _bdot(et[pg * GROUP_LANES:(pg + 1) * GROUP_LANES, :], kvbuf[slot, pg])
        return acc, lsum

    acc, lsum = lax.fori_loop(0, n_chunks, v_chunk,
                              (jnp.zeros((GROUP_LANES, ATT_HD), F32), jnp.zeros((LANES, 1), F32)))
    top = lambda r: jnp.concatenate([r, jnp.zeros((LANES - r.shape[0], LANES), F32)], axis=0)
    et_new = top(new_pairs * e_new).T
    acc = acc + _bdot(et_new[0:GROUP_LANES, :], top(vn_ref[...]))
    lsum = lsum + jnp.sum(et_new, axis=1, keepdims=True)
    l16 = lsum[0:GROUP_LANES, :]
    for pg in range(1, npc):
        l16 = l16 + lsum[pg * GROUP_LANES:(pg + 1) * GROUP_LANES, :]
    o_ref[...] = (acc / l16).astype(BF16)


def _dsa_step(z2d, cache_k, cache_v, cache_kidx, page_table, *, d_model):
    off = _odd_columns(d_model)
    bsz = z2d.shape[0]
    n_pages = page_table.shape[1]
    n_phys, page, kvh, hd = cache_k.shape
    n_kv = kvh * hd
    heads = d_model // hd
    assert page == PAGE_SIZE and heads == GROUP_LANES and n_pages % PAGES_PER_CHUNK == 0
    n_chunks = n_pages // PAGES_PER_CHUNK
    topk = min(INDEX_TOPK, (n_pages * page + 1) // 4)
    q = z2d[:, off["q"]:off["q"] + d_model].reshape(bsz, kvh, heads // kvh, hd)
    qi = z2d[:, off["qi"]:off["qi"] + IDX_HEADS * IDX_DIM].reshape(bsz, IDX_HEADS, IDX_DIM)
    tile = lambda a: jnp.tile(a, (1, 1, PAGES_PER_CHUNK))
    qit = tile(jnp.transpose(qi, (0, 2, 1)))
    wq = tile(jnp.transpose(q.reshape(bsz, heads, hd), (0, 2, 1)))
    wl = tile(z2d[:, off["wi"]:off["wi"] + IDX_HEADS].reshape(bsz, 1, IDX_HEADS))
    kin = z2d[:, off["ki"]:off["ki"] + IDX_DIM].reshape(bsz, 1, IDX_DIM)
    new_rows = lambda c: jnp.pad(z2d[:, c:c + n_kv].reshape(bsz, kvh, hd), ((0, 0), (0, 8 - kvh), (0, 0)))
    kn, vn = new_rows(off["k"]), new_rows(off["v"])
    expand = jnp.repeat(jnp.eye(page, dtype=BF16), kvh, axis=0)
    per_seq = lambda a: pl.BlockSpec((None,) + a.shape[1:], lambda b, pt: (b, 0, 0))
    hbm = pl.BlockSpec(memory_space=pl.ANY)
    grid_spec = pltpu.PrefetchScalarGridSpec(
        num_scalar_prefetch=1,
        grid=(bsz,),
        in_specs=[per_seq(qit), per_seq(wq), per_seq(wl), per_seq(kin), per_seq(kn), per_seq(vn),
                  pl.BlockSpec(expand.shape, lambda b, pt: (0, 0)), hbm, hbm, hbm],
        out_specs=pl.BlockSpec((None, heads, hd), lambda b, pt: (b, 0, 0)),
        scratch_shapes=[
            pltpu.VMEM((2, PAGES_PER_CHUNK, page, IDX_DIM), F32),
            pltpu.VMEM((2, PAGES_PER_CHUNK, page * kvh, hd), F32),
            pltpu.SemaphoreType.DMA((2,)),
            pltpu.SemaphoreType.DMA((2,)),
            pltpu.VMEM((PAGES_PER_CHUNK, IDX_DIM, LANES), BF16),
            pltpu.VMEM((PAGES_PER_CHUNK, hd, LANES), BF16),
            pltpu.VMEM((n_chunks, page, LANES), jnp.int32),
            pltpu.VMEM((n_chunks, page * kvh, LANES), F32),
        ])
    o = pl.pallas_call(
        functools.partial(_dsa_step_kernel, topk=topk),
        grid_spec=grid_spec,
        out_shape=jax.ShapeDtypeStruct((bsz, heads, hd), BF16),
        compiler_params=_params(("arbitrary",), 40),
        name="dsa_step",
    )(page_table, qit, wq, wl, kin, kn, vn, expand, cache_kidx, cache_k.reshape(n_phys, page * kvh, hd),
      cache_v.reshape(n_phys, page * kvh, hd))
    return o.reshape(bsz, d_model)


def _dsa_prompt(z2d, *, batch, seq, d_model, qb):
    off = _odd_columns(d_model)
    nq = seq // qb
    n_kv = ATT_KV_HEADS * ATT_HD
    topk = min(INDEX_TOPK, seq // 4)
    rows = lambda w, c: pl.BlockSpec((qb, w), lambda b, i: (b * nq + i, c // w))
    whole = lambda w, c: pl.BlockSpec((seq, w), lambda b, i: (b, c // w))
    return pl.pallas_call(
        functools.partial(_dsa_kernel, topk=topk),
        grid=(batch, nq),
        in_specs=[rows(d_model, off["q"]), rows(IDX_HEADS * IDX_DIM, off["qi"]), rows(LANES, off["wi"]),
                  whole(IDX_DIM, off["ki"]), whole(n_kv, off["k"]), whole(n_kv, off["v"])],
        out_specs=pl.BlockSpec((qb, d_model), lambda b, i: (b * nq + i, 0)),
        out_shape=jax.ShapeDtypeStruct((batch * seq, d_model), BF16),
        scratch_shapes=[pltpu.VMEM((seq, IDX_DIM), BF16), pltpu.VMEM((seq, n_kv), BF16), pltpu.VMEM((seq, n_kv), BF16),
                        pltpu.VMEM((qb, seq), jnp.int32)],
        compiler_params=_params(("parallel", "arbitrary"), 56),
        name="dsa_prompt",
    )(z2d, z2d, z2d, z2d, z2d, z2d)


def _trunk(x2d, mods, wts, cfg, even_fn, odd_fn):
    tm, tf, tpb = cfg["tm"], cfg["tf"], cfg["tpb"]
    ffn = functools.partial(_ffn, tm=tm, tf=tf, tiles_per_batch=tpb, vmem_mib=cfg["ffn_vmem"])
    x = x2d
    states = []
    depth = wts["norm_g"].shape[0]
    for layer in range(depth):
        g, mod = wts["norm_g"][layer], mods[layer]
        x = ffn(x, g[0], mod, 0, wts["w_ffn_in"], wts["w_ffn_out"], (layer, 0), wts["final_g"], final_norm=False)
        x, st = (even_fn if layer % 2 == 0 else odd_fn)(x, g[1], mod)
        states.append(st)
        x = ffn(x, g[2], mod, 2, wts["w_ffn_in"], wts["w_ffn_out"], (layer, 1), wts["final_g"],
                final_norm=layer == depth - 1)
    return x, states


def kernel(x_prompt, x_sample, c_prompt, c_sample, state_ret, state_s5_re, state_s5_im, cache_k, cache_v, cache_kidx,
           page_table, norm_g, w_ada, b_ada, w_ffn_in, w_ffn_out, w_in_even, w_out_even, s5_a_re, s5_a_im, s5_log_dt,
           s5_b_re, s5_b_im, s5_c_re, s5_c_im, s5_d, w_glu, b_glu, w_in_odd, w_out_odd, final_g):
    bp, lp, d = x_prompt.shape
    bs, ls, _ = x_sample.shape
    assert ls == 1
    depth = w_ada.shape[0]
    n_mod = N_SUBLAYERS * 3

    rows = -(-(bp + bs) // 8) * 8
    c_all = jnp.concatenate([c_prompt, c_sample, jnp.zeros((rows - bp - bs, d), F32)], axis=0)
    mod_all = _ada_mod(c_all, w_ada, b_ada)
    mods_p = [mod_all[l, :bp].reshape(bp, n_mod, 1, d) for l in range(depth)]
    mods_s = [jnp.transpose(mod_all[l, bp:bp + bs].reshape(bs, n_mod, d), (1, 0, 2))[None] for l in range(depth)]

    p = dict(w_in_even=w_in_even, w_out_even=w_out_even, s5_a_re=s5_a_re, s5_a_im=s5_a_im, s5_log_dt=s5_log_dt,
             s5_b_re=s5_b_re, s5_b_im=s5_b_im, s5_c_re=s5_c_re, s5_c_im=s5_c_im, s5_d=s5_d, w_glu=w_glu, b_glu=b_glu)
    sp = _s5_prepare(p)
    w_odd = _odd_weight(w_in_odd, d)
    off = _odd_columns(d)
    n_kv = ATT_KV_HEADS * ATT_HD
    wts = dict(norm_g=norm_g, w_ffn_in=w_ffn_in, w_ffn_out=w_ffn_out, final_g=final_g)
    odd_tn = 768

    def cache_parts(z, b, l):
        return (z[:, off["k"]:off["k"] + n_kv].reshape(b, l, ATT_KV_HEADS, ATT_HD),
                z[:, off["v"]:off["v"] + n_kv].reshape(b, l, ATT_KV_HEADS, ATT_HD),
                z[:, off["ki"]:off["ki"] + IDX_DIM].reshape(b, l, IDX_DIM))

    cfg_p = dict(tm=min(1024, lp), tf=256, ffn_vmem=60)
    cfg_p["tpb"] = lp // cfg_p["tm"]

    def even_p(x, g, mod):
        zero_s5 = jnp.zeros((bp,) + state_s5_re.shape[1:], F32)
        x, s_ret, sr, si = _even_layer(x, g, mod, p, jnp.zeros((bp,) + state_ret.shape[1:], F32), zero_s5, zero_s5,
                                       batch=bp, seq=lp, pos0=0, tm=cfg_p["tm"], tiles_per_batch=cfg_p["tpb"],
                                       rb=min(256, lp), chunk=min(RET_CHUNK, lp), s5_t=min(128, lp), sp=sp)
        return x, (s_ret, sr, si)

    def odd_p(x, g, mod):
        z = _proj(x, g, mod, 1, w_odd, tm=cfg_p["tm"], tn=odd_tn, tiles_per_batch=cfg_p["tpb"], vmem_mib=52)
        o = _dsa_prompt(z, batch=bp, seq=lp, d_model=d, qb=min(128, lp))
        x = _outproj([(o, d, None)], w_out_odd, x, mod, tm=cfg_p["tm"], tn=1024, tiles_per_batch=cfg_p["tpb"],
                     vmem_mib=48)
        return x, cache_parts(z, bp, lp)

    y_p, (ev_p, od_p) = _trunk(x_prompt.reshape(bp * lp, d), mods_p, wts, cfg_p, even_p, odd_p)

    past_len = page_table.shape[1] * cache_k.shape[1]
    cfg_s = dict(tm=bs, tf=512, tpb=1, ffn_vmem=40)

    def even_s(x, g, mod):
        x, s_ret, sr, si = _even_layer(x, g, mod, p, state_ret, state_s5_re, state_s5_im, batch=bs, seq=1,
                                       pos0=past_len, tm=bs, tiles_per_batch=1, rb=1, chunk=1, s5_t=1, sp=sp)
        return x, (s_ret, sr, si)

    def odd_s(x, g, mod):
        z = _proj(x, g, mod, 1, w_odd, tm=bs, tn=odd_tn, tiles_per_batch=1, vmem_mib=40)
        o = _dsa_step(z, cache_k, cache_v, cache_kidx, page_table, d_model=d)
        x = _outproj([(o, d, None)], w_out_odd, x, mod, tm=bs, tn=1024, tiles_per_batch=1, vmem_mib=40)
        return x, cache_parts(z, bs, 1)

    y_s, (ev_s, od_s) = _trunk(x_sample.reshape(bs, d), mods_s, wts, cfg_s, even_s, odd_s)

    return (y_p.reshape(bp, lp, d), y_s.reshape(bs, 1, d), *ev_p, *od_p, *ev_s, *od_s)
```

```python
import functools
import math

import jax
import jax.numpy as jnp
from jax import lax
from jax.experimental import pallas as pl
from jax.experimental.pallas import tpu as pltpu

F32 = jnp.float32
BF16 = jnp.bfloat16

EPS = 1e-6
FFN_RES = 0.5
N_SUBLAYERS = 3
RET_HEADS = 4
RET_CHUNK = 128
ROPE_BASE = 10000.0
S5_GROUP_CH = 16
S5_STATE = 64
ATT_HD = 128
ATT_KV_HEADS = 4
IDX_HEADS = 16
IDX_DIM = 128
INDEX_TOPK = 256
PAGE_SIZE = 128

LANES = 128
MIB = 1024 * 1024
NEG_INF = float("-inf")


def _params(sem, vmem_mib):
    return pltpu.CompilerParams(dimension_semantics=sem, vmem_limit_bytes=vmem_mib * MIB)


def _bdot(a, b):
    return jnp.dot(a.astype(BF16), b.astype(BF16), preferred_element_type=F32)


def _bdot_nt(a, b):
    return lax.dot_general(a.astype(BF16), b.astype(BF16), (((1,), (1,)), ((), ())),
                           preferred_element_type=F32)


def _bdot_tn(a, b):
    return lax.dot_general(a.astype(BF16), b.astype(BF16), (((0,), (0,)), ((), ())),
                           preferred_element_type=F32)


def _silu(x):
    return x * jax.nn.sigmoid(x)


def _rms(x, g):
    ms = jnp.mean(x * x, axis=-1, keepdims=True)
    return x * lax.rsqrt(ms + EPS) * g


def _row_chunks(n_rows, chunk, fn):
    if n_rows <= chunk:
        fn(slice(0, n_rows))
        return
    assert n_rows % chunk == 0

    def body(i, c):
        fn(pl.ds(pl.multiple_of(i * chunk, chunk), chunk))
        return c

    lax.fori_loop(0, n_rows // chunk, body, 0)


ROW_CHUNK = 256


def _norm_mod_to(h_ref, x_ref, g_ref, sc_ref, sh_ref):
    n_rows = x_ref.shape[0]
    assert sc_ref.shape[0] == 1 or n_rows <= ROW_CHUNK

    def fn(rows):
        y = _rms(x_ref[rows, :], g_ref[...])
        h_ref[rows, :] = (y * (1.0 + sc_ref[...]) + sh_ref[...]).astype(BF16)

    _row_chunks(n_rows, ROW_CHUNK, fn)


def _ada_kernel(c_ref, w_ref, b_ref, o_ref):
    o_ref[...] = _bdot(_silu(c_ref[...]), w_ref[...]) + b_ref[...]


def _ada_mod(c_all, w_ada, b_ada):
    depth, d, n = w_ada.shape
    rows = c_all.shape[0]
    tn = 1024
    return pl.pallas_call(
        _ada_kernel,
        grid=(depth, n // tn),
        in_specs=[
            pl.BlockSpec((rows, d), lambda l, j: (0, 0)),
            pl.BlockSpec((None, d, tn), lambda l, j: (l, 0, j)),
            pl.BlockSpec((None, 1, tn), lambda l, j: (l, 0, j)),
        ],
        out_specs=pl.BlockSpec((None, rows, tn), lambda l, j: (l, 0, j)),
        out_shape=jax.ShapeDtypeStruct((depth, rows, n), F32),
        compiler_params=_params(("arbitrary", "arbitrary"), 40),
        name="ada_mod",
    )(c_all, w_ada, b_ada.reshape(depth, 1, n))


def _mod_specs(mod, sub, tiles_per_batch, which):
    _, _, r, d = mod.shape
    specs = []
    for k in which:
        idx = sub * 3 + k
        specs.append(pl.BlockSpec((None, None, r, d),
                                  lambda i, j, idx=idx: (i // tiles_per_batch, idx, 0, 0)))
    return specs


def _ffn_kernel(x_ref, g_ref, sh_ref, sc_ref, gt_ref, wa_ref, wb_ref, wo_ref, fg_ref, o_ref, h_ref,
                *, final_norm):
    j = pl.program_id(1)
    nj = pl.num_programs(1)

    @pl.when(j == 0)
    def _():
        _norm_mod_to(h_ref, x_ref, g_ref, sc_ref, sh_ref)
        o_ref[...] = jnp.zeros(o_ref.shape, F32)

    h = h_ref[...]
    a = jnp.dot(h, wa_ref[...].astype(BF16), preferred_element_type=F32)
    b = jnp.dot(h, wb_ref[...].astype(BF16), preferred_element_type=F32)
    o_ref[...] += _bdot(_silu(a) * b, wo_ref[...])

    @pl.when(j == nj - 1)
    def _():
        def fn(rows):
            y = x_ref[rows, :] + (FFN_RES * gt_ref[...]) * o_ref[rows, :]
            if final_norm:
                y = _rms(y, fg_ref[...])
            o_ref[rows, :] = y

        _row_chunks(x_ref.shape[0], ROW_CHUNK, fn)


def _ffn(x2d, norm_g, mod, sub, w_in, w_out, wsel, final_g, *, tm, tf, tiles_per_batch, final_norm, vmem_mib):
    m, d = x2d.shape
    f = w_out.shape[2]
    nf = f // tf
    assert m % tm == 0 and f % tf == 0
    row = lambda i, j: (0, 0)
    l, k = wsel
    return pl.pallas_call(
        functools.partial(_ffn_kernel, final_norm=final_norm),
        grid=(m // tm, nf),
        in_specs=[
            pl.BlockSpec((tm, d), lambda i, j: (i, 0)),
            pl.BlockSpec((1, d), row),
            *_mod_specs(mod, sub, tiles_per_batch, (0, 1, 2)),
            pl.BlockSpec((None, None, d, tf), lambda i, j: (l, k, 0, j)),
            pl.BlockSpec((None, None, d, tf), lambda i, j: (l, k, 0, nf + j)),
            pl.BlockSpec((None, None, tf, d), lambda i, j: (l, k, j, 0)),
            pl.BlockSpec((1, d), row),
        ],
        out_specs=pl.BlockSpec((tm, d), lambda i, j: (i, 0)),
        out_shape=jax.ShapeDtypeStruct((m, d), F32),
        scratch_shapes=[pltpu.VMEM((tm, d), BF16)],
        compiler_params=_params(("parallel", "arbitrary"), vmem_mib),
        name="ffn",
    )(x2d, norm_g.reshape(1, d), mod, mod, mod, w_in, w_in, w_out, final_g.reshape(1, d))


def _proj_kernel(x_ref, g_ref, sh_ref, sc_ref, w_ref, o_ref, h_ref):
    @pl.when(pl.program_id(1) == 0)
    def _():
        _norm_mod_to(h_ref, x_ref, g_ref, sc_ref, sh_ref)

    o_ref[...] = jnp.dot(h_ref[...], w_ref[...].astype(BF16), preferred_element_type=F32)


def _proj(x2d, norm_g, mod, sub, w, *, tm, tn, tiles_per_batch, vmem_mib):
    m, d = x2d.shape
    n = w.shape[1]
    assert m % tm == 0 and n % tn == 0
    return pl.pallas_call(
        _proj_kernel,
        grid=(m // tm, n // tn),
        in_specs=[
            pl.BlockSpec((tm, d), lambda i, j: (i, 0)),
            pl.BlockSpec((1, d), lambda i, j: (0, 0)),
            *_mod_specs(mod, sub, tiles_per_batch, (0, 1)),
            pl.BlockSpec((d, tn), lambda i, j: (0, j)),
        ],
        out_specs=pl.BlockSpec((tm, tn), lambda i, j: (i, j)),
        out_shape=jax.ShapeDtypeStruct((m, n), F32),
        scratch_shapes=[pltpu.VMEM((tm, d), BF16)],
        compiler_params=_params(("parallel", "arbitrary"), vmem_mib),
        name="mixer_in_proj",
    )(x2d, norm_g.reshape(1, d), mod, mod, w)


def _outproj_kernel(*refs, n_parts):
    a_refs = refs[:n_parts]
    w_ref, x_ref, gt_ref, o_ref = refs[n_parts:]
    acc = None
    k0 = 0
    for a_ref in a_refs:
        kk = a_ref.shape[1]
        p = jnp.dot(a_ref[...], w_ref[k0:k0 + kk, :].astype(BF16), preferred_element_type=F32)
        acc = p if acc is None else acc + p
        k0 += kk
    o_ref[...] = x_ref[...] + gt_ref[...] * acc


def _outproj(a_parts, w, x2d, mod, *, tm, tn, tiles_per_batch, vmem_mib):
    m, d = x2d.shape
    k = w.shape[0]
    assert sum(kw for _, kw, _ in a_parts) == k and m % tm == 0 and d % tn == 0
    _, _, r, _ = mod.shape
    return pl.pallas_call(
        functools.partial(_outproj_kernel, n_parts=len(a_parts)),
        grid=(m // tm, d // tn),
        in_specs=[
            *[pl.BlockSpec((tm, kw), imap or (lambda i, j: (i, 0))) for _, kw, imap in a_parts],
            pl.BlockSpec((k, tn), lambda i, j: (0, j)),
            pl.BlockSpec((tm, tn), lambda i, j: (i, j)),
            pl.BlockSpec((None, None, r, tn), lambda i, j: (i // tiles_per_batch, 5, 0, j)),
        ],
        out_specs=pl.BlockSpec((tm, tn), lambda i, j: (i, j)),
        out_shape=jax.ShapeDtypeStruct((m, d), F32),
        compiler_params=_params(("parallel", "arbitrary"), vmem_mib),
        name="mixer_out_proj",
    )(*[a for a, _, _ in a_parts], w, x2d, mod)


def _rot(x1, x2, cos, sin):
    return jnp.concatenate([x1 * cos - x2 * sin, x1 * sin + x2 * cos], axis=-1)


def _head_norm(o):
    mu = jnp.mean(o, axis=-1, keepdims=True)
    var = jnp.mean(jnp.square(o - mu), axis=-1, keepdims=True)
    return (o - mu) * lax.rsqrt(var + 1e-5)


def _ret_log_g(h):
    return math.log1p(-(2.0 ** (-5.0 - h)))


def _ret_kernel(q_ref, k_ref, v_ref, g_ref, cos_ref, sin_ref, s0_ref, o_ref, s_ref, *, chunk):
    c = pl.program_id(1)
    rb = q_ref.shape[0]
    dk = q_ref.shape[1] // RET_HEADS
    half = dk // 2

    @pl.when(c == 0)
    def _():
        s_ref[...] = s0_ref[...]

    ri = lax.broadcasted_iota(jnp.int32, (chunk, chunk), 0).astype(F32)
    ci = lax.broadcasted_iota(jnp.int32, (chunk, chunk), 1).astype(F32)
    diff = ri - ci
    rowf = lax.broadcasted_iota(jnp.int32, (chunk, dk), 0).astype(F32)
    for h in range(RET_HEADS):
        lg = _ret_log_g(h)
        dmat = jnp.where(diff >= 0, jnp.exp(jnp.maximum(diff, 0.0) * lg), 0.0)
        q_dec = jnp.exp((rowf + 1.0) * lg)
        k_dec = jnp.exp((chunk - 1.0 - rowf) * lg)
        c_dec = math.exp(chunk * lg)
        c0 = h * dk
        for ck in range(rb // chunk):
            rows = slice(ck * chunk, (ck + 1) * chunk)
            cos = cos_ref[rows, :]
            sin = sin_ref[rows, :]
            qr = _rot(q_ref[rows, c0:c0 + half], q_ref[rows, c0 + half:c0 + dk], cos, sin)
            kr = _rot(k_ref[rows, c0:c0 + half], k_ref[rows, c0 + half:c0 + dk], cos, sin) * (dk ** -0.5)
            vb = v_ref[rows, c0:c0 + dk].astype(BF16)
            qb = qr.astype(BF16)
            a = _bdot_nt(qb, kr) * dmat
            s = s_ref[h]
            o = _bdot(a, vb) + _bdot(qb, s) * q_dec
            s_ref[h] = s * c_dec + _bdot_tn(kr * k_dec, vb)
            o_ref[rows, c0:c0 + dk] = (_head_norm(o) * _silu(g_ref[rows, c0:c0 + dk])).astype(BF16)


def _retention(z2d, col0, cos, sin, state0, *, batch, seq, rb, chunk):
    _, h, dk, dv = state0.shape
    w = h * dk
    nb = seq // rb
    assert seq % rb == 0 and rb % chunk == 0 and col0 % w == 0
    cb = col0 // w
    zspec = lambda k: pl.BlockSpec((rb, w), lambda b, c, k=k: (b * nb + c, cb + k))
    tab = pl.BlockSpec((rb, dk // 2), lambda b, c: (c, 0))
    st = pl.BlockSpec((None, h, dk, dv), lambda b, c: (b, 0, 0, 0))
    return pl.pallas_call(
        functools.partial(_ret_kernel, chunk=chunk),
        grid=(batch, nb),
        in_specs=[zspec(0), zspec(1), zspec(2), zspec(3), tab, tab, st],
        out_specs=[pl.BlockSpec((rb, w), lambda b, c: (b * nb + c, 0)), st],
        out_shape=[jax.ShapeDtypeStruct((batch * seq, w), BF16),
                   jax.ShapeDtypeStruct(state0.shape, F32)],
        compiler_params=_params(("parallel", "arbitrary"), 40),
        name="retention",
    )(z2d, z2d, z2d, z2d, cos, sin, state0)


def _rope_tables(pos, half):
    freqs = ROPE_BASE ** (-jnp.arange(half, dtype=F32) / half)
    ang = pos.astype(F32)[:, None] * freqs[None, :]
    return jnp.cos(ang), jnp.sin(ang)


def _r16(x):
    return x.astype(BF16).astype(F32)


def _ret_step_kernel(q_ref, krow_ref, kcol_ref, v_ref, g_ref, cos_ref, sin_ref, cosc_ref, sinc_ref, s0_ref,
                     o_ref, s_ref):
    dk = q_ref.shape[1] // RET_HEADS
    half = dk // 2
    cos, sin = cos_ref[...], sin_ref[...]
    cosc, sinc = cosc_ref[...], sinc_ref[...]
    for h in range(RET_HEADS):
        dec = math.exp(_ret_log_g(h))
        c0 = h * dk
        qr = _r16(_rot(q_ref[:, c0:c0 + half], q_ref[:, c0 + half:c0 + dk], cos, sin))
        kr = _r16(_rot(krow_ref[:, c0:c0 + half], krow_ref[:, c0 + half:c0 + dk], cos, sin) * (dk ** -0.5))
        k1, k2 = kcol_ref[h, 0:half, :], kcol_ref[h, half:dk, :]
        kc = _r16(jnp.concatenate([k1 * cosc - k2 * sinc, k1 * sinc + k2 * cosc], axis=0) * (dk ** -0.5))
        v = _r16(v_ref[:, c0:c0 + dk])
        a = jnp.sum(qr * kr, axis=-1, keepdims=True)
        s = s0_ref[h]
        o = _r16(a) * v + _bdot(qr, s) * dec
        s_ref[h] = s * dec + kc * v
        o_ref[:, c0:c0 + dk] = (_head_norm(o) * _silu(g_ref[:, c0:c0 + dk])).astype(BF16)


def _retention_step(z2d, cos, sin, state0):
    b, h, dk, dv = state0.shape
    w = h * dk
    z3 = z2d.reshape(b, 1, z2d.shape[1])
    kcol = z2d[:, w:2 * w].reshape(b, h, dk, 1)
    half = dk // 2
    zspec = lambda k: pl.BlockSpec((None, 1, w), lambda i, k=k: (i, 0, k))
    row = pl.BlockSpec((1, half), lambda i: (0, 0))
    col = pl.BlockSpec((half, 1), lambda i: (0, 0))
    st = pl.BlockSpec((None, h, dk, dv), lambda i: (i, 0, 0, 0))
    og, s = pl.pallas_call(
        _ret_step_kernel,
        grid=(b,),
        in_specs=[zspec(0), zspec(1), pl.BlockSpec((None, h, dk, 1), lambda i: (i, 0, 0, 0)), zspec(2), zspec(3),
                  row, row, col, col, st],
        out_specs=[pl.BlockSpec((None, 1, w), lambda i: (i, 0, 0)), st],
        out_shape=[jax.ShapeDtypeStruct((b, 1, w), BF16), jax.ShapeDtypeStruct(state0.shape, F32)],
        compiler_params=_params(("parallel",), 32),
        name="retention_step",
    )(z3, z3, kcol, z3, z3, cos, sin, cos.reshape(half, 1), sin.reshape(half, 1), state0)
    return og.reshape(b, w), s


S5_BLOCKS = 8


def _s5_disc(ar, ai, dt):
    mag = jnp.exp(dt * ar)
    abr = mag * jnp.cos(dt * ai)
    abi = mag * jnp.sin(dt * ai)
    den = ar * ar + ai * ai
    xr = abr - 1.0
    return abr, abi, (xr * ar + abi * ai) / den, (abi * ar - xr * ai) / den


def _s5_param_kernel(ar_ref, ai_ref, ldt_ref, arr_ref, air_ref, br_ref, bi_ref, abr_ref, abi_ref, bbr_ref, bbi_ref):
    dt = jnp.exp(ldt_ref[...])
    abr_ref[...], abi_ref[...], _, _ = _s5_disc(ar_ref[...], ai_ref[...], dt)
    _, _, fr, fi = _s5_disc(arr_ref[...], air_ref[...], dt)
    br, bi = br_ref[...], bi_ref[...]
    bbr_ref[...] = fr * br - fi * bi
    bbi_ref[...] = fr * bi + fi * br


def _s5_prepare(p):
    g, st = p["s5_a_re"].shape
    ch = p["s5_d"].shape[1]
    rep = lambda a: jnp.repeat(a, ch, axis=1)
    abr, abi, bbr, bbi = pl.pallas_call(
        _s5_param_kernel,
        out_shape=[jax.ShapeDtypeStruct((g, st), F32)] * 2 + [jax.ShapeDtypeStruct((g, st * ch), F32)] * 2,
        name="s5_discretise",
    )(p["s5_a_re"], p["s5_a_im"], p["s5_log_dt"].reshape(g, 1), rep(p["s5_a_re"]), rep(p["s5_a_im"]),
      p["s5_b_re"].reshape(g, st * ch), p["s5_b_im"].reshape(g, st * ch))
    nb = S5_BLOCKS
    gl = g // nb
    eye = jnp.eye(gl, dtype=F32)
    w_in = lambda bb: jnp.einsum("bgpc,gh->bgchp", bb.reshape(nb, gl, st, ch), eye).reshape(nb, gl * ch, gl * st)
    w_out = lambda c: jnp.einsum("bgcp,gh->bgphc", c.reshape(nb, gl, ch, st), eye).reshape(nb, gl * st, gl * ch)
    return dict(
        abr=abr.reshape(1, g * st), abi=abi.reshape(1, g * st),
        wbr=w_in(bbr).astype(BF16), wbi=w_in(bbi).astype(BF16),
        wcr=w_out(p["s5_c_re"]).astype(BF16), wci=w_out(p["s5_c_im"]).astype(BF16),
        d=p["s5_d"].reshape(1, g * ch), w_glu=p["w_glu"], b_glu=p["b_glu"].reshape(1, -1))


S5_SCAN_LANES = 1024


def _s5_kernel(u_ref, x0r_ref, x0i_ref, abr_ref, abi_ref, wbr_ref, wbi_ref, wcr_ref, wci_ref, d_ref, wg_ref,
               bg_ref, y_ref, xr_ref, xi_ref, bur, bui, yb, wgb, *, ns, t):
    nblk = wbr_ref.shape[0]
    uw, sw = wbr_ref.shape[1], wbr_ref.shape[2]
    n_tiles = nblk * sw // LANES
    sub = 8

    @pl.when(pl.program_id(0) == 0)
    def _():
        xr_ref[...] = x0r_ref[...]
        xi_ref[...] = x0i_ref[...]
        wgb[...] = wg_ref[...].astype(BF16)

    tiles_per_blk = sw // LANES
    lane_tile = lambda j: slice(j * LANES, (j + 1) * LANES)
    for b in range(nblk):
        ub = u_ref[:, b * uw:(b + 1) * uw].astype(BF16)
        pr = jnp.dot(ub, wbr_ref[b], preferred_element_type=F32)
        pi = jnp.dot(ub, wbi_ref[b], preferred_element_type=F32)
        for k in range(tiles_per_blk):
            bur[b * tiles_per_blk + k] = pr[:, lane_tile(k)]
            bui[b * tiles_per_blk + k] = pi[:, lane_tile(k)]

    def affine(ar, ai, xr, xi, vr, vi):
        return ar * xr - ai * xi + vr, ar * xi + ai * xr + vi

    per_pass = S5_SCAN_LANES // LANES
    for j0 in range(0, n_tiles, per_pass):
        tiles = list(range(j0, min(j0 + per_pass, n_tiles)))
        if t == 1:
            for j in tiles:
                ar = jnp.broadcast_to(abr_ref[:, lane_tile(j)], (ns, LANES))
                ai = jnp.broadcast_to(abi_ref[:, lane_tile(j)], (ns, LANES))
                nr, ni = affine(ar, ai, xr_ref[:, lane_tile(j)], xi_ref[:, lane_tile(j)], bur[j], bui[j])
                bur[j], bui[j] = nr, ni
                xr_ref[:, lane_tile(j)], xi_ref[:, lane_tile(j)] = nr, ni
            continue
        assert 2 * ns == sub and t % 2 == 0
        ar = [jnp.broadcast_to(abr_ref[:, lane_tile(j)], (sub, LANES)) for j in tiles]
        ai = [jnp.broadcast_to(abi_ref[:, lane_tile(j)], (sub, LANES)) for j in tiles]
        top = lax.broadcasted_iota(jnp.int32, (sub, LANES), 0) < ns
        swap = lambda x: pltpu.roll(x, ns, axis=0)

        def step(i, carry, tiles=tiles, ar=ar, ai=ai):
            rows = pl.ds(pl.multiple_of(i * sub, sub), sub)
            out = []
            for k, j in enumerate(tiles):
                pr_, pi_ = carry[k]
                vr, vi = bur[j, rows, :], bui[j, rows, :]
                er, ei = affine(ar[k], ai[k], swap(pr_), swap(pi_), vr, vi)
                orr, oi = affine(ar[k], ai[k], swap(er), swap(ei), vr, vi)
                nr, ni = jnp.where(top, er, orr), jnp.where(top, ei, oi)
                bur[j, rows, :] = nr
                bui[j, rows, :] = ni
                out.append((nr, ni))
            return tuple(out)

        twice = lambda x: jnp.concatenate([x, x], axis=0)
        fin = lax.fori_loop(0, t // 2, step,
                            tuple((twice(xr_ref[:, lane_tile(j)]), twice(xi_ref[:, lane_tile(j)])) for j in tiles))
        for k, j in enumerate(tiles):
            xr_ref[:, lane_tile(j)] = fin[k][0][ns:, :]
            xi_ref[:, lane_tile(j)] = fin[k][1][ns:, :]

    blk_states = lambda ref, b: jnp.concatenate(
        [ref[b * tiles_per_blk + k] for k in range(tiles_per_blk)], axis=1).astype(BF16)
    for b in range(nblk):
        yb[:, b * uw:(b + 1) * uw] = (jnp.dot(blk_states(bur, b), wcr_ref[b], preferred_element_type=F32)
                                      - jnp.dot(blk_states(bui, b), wci_ref[b], preferred_element_type=F32))
    gl = jax.nn.gelu(yb[...] + d_ref[...] * u_ref[...])
    gate = jax.nn.sigmoid(jnp.dot(gl.astype(BF16), wgb[...], preferred_element_type=F32) + bg_ref[...])
    y_ref[...] = (gl * gate).astype(BF16)


def _s5(u2d, col_blk, sp, x0r, x0i, *, batch, t):
    w = sp["d"].shape[1]
    n_state = sp["abr"].shape[1]
    rows = batch * t
    assert u2d.shape[0] % rows == 0
    full = lambda a: pl.BlockSpec(a.shape, lambda c, nd=a.ndim: (0,) * nd)
    st_spec = pl.BlockSpec((batch, n_state), lambda c: (0, 0))
    consts = [sp["abr"], sp["abi"], sp["wbr"], sp["wbi"], sp["wcr"], sp["wci"], sp["d"], sp["w_glu"], sp["b_glu"]]
    return pl.pallas_call(
        functools.partial(_s5_kernel, ns=batch, t=t),
        grid=(u2d.shape[0] // rows,),
        in_specs=[pl.BlockSpec((rows, w), lambda c: (c, col_blk)), st_spec, st_spec] + [full(a) for a in consts],
        out_specs=[pl.BlockSpec((rows, w), lambda c: (c, 0)), st_spec, st_spec],
        out_shape=[jax.ShapeDtypeStruct((u2d.shape[0], w), BF16), jax.ShapeDtypeStruct((batch, n_state), F32),
                   jax.ShapeDtypeStruct((batch, n_state), F32)],
        scratch_shapes=[pltpu.VMEM((n_state // LANES, rows, LANES), F32), pltpu.VMEM((n_state // LANES, rows, LANES), F32),
                        pltpu.VMEM((rows, w), F32), pltpu.VMEM(sp["w_glu"].shape, BF16)],
        compiler_params=_params(("arbitrary",), 48),
        name="s5",
    )(u2d, x0r.reshape(batch, n_state), x0i.reshape(batch, n_state), *consts)


def _even_layer(x2d, norm_g, mod, p, ret0, s5r0, s5i0, *, batch, seq, pos0, tm, tiles_per_batch, rb, chunk, s5_t,
                sp=None):
    _, h, dk, _ = ret0.shape
    rw = h * dk
    z = _proj(x2d, norm_g, mod, 1, p["w_in_even"], tm=tm, tn=1024, tiles_per_batch=tiles_per_batch, vmem_mib=52)
    cos, sin = _rope_tables(pos0 + jnp.arange(seq), dk // 2)
    if seq > 1:
        og, s_ret = _retention(z, 0, cos, sin, ret0, batch=batch, seq=seq, rb=rb, chunk=chunk)
    else:
        og, s_ret = _retention_step(z, cos, sin, ret0)
    sp = _s5_prepare(p) if sp is None else sp
    sw = sp["d"].shape[1]
    if seq > 1:
        u2d = jnp.transpose(z.reshape(batch, seq, -1)[:, :, 4 * rw:4 * rw + sw], (1, 0, 2)).reshape(seq * batch, sw)
        y, sr, si = _s5(u2d, 0, sp, s5r0, s5i0, batch=batch, t=s5_t)
        y_part = (y.reshape(seq, batch * sw), sw, lambda i, j: (i % tiles_per_batch, i // tiles_per_batch))
    else:
        y, sr, si = _s5(z, (4 * rw) // sw, sp, s5r0, s5i0, batch=batch, t=1)
        y_part = (y, sw, None)
    x_new = _outproj([(og, rw, None), y_part], p["w_out_even"], x2d, mod, tm=tm, tn=1024,
                     tiles_per_batch=tiles_per_batch, vmem_mib=48)
    return x_new, s_ret, sr.reshape(s5r0.shape), si.reshape(s5i0.shape)


INT32_MIN = -2 ** 31


def _sortable_keys(score):
    score = jnp.where(score == 0.0, 0.0, score)
    bits = lax.bitcast_convert_type(score, jnp.int32)
    return bits ^ (lax.shift_right_arithmetic(bits, 31) & 0x7FFFFFFF)


def _kth_largest(count_ge, shape, k):
    kf = jnp.float32(k)
    prefix = jnp.where(count_ge(jnp.zeros(shape, jnp.int32)) >= kf, 0, INT32_MIN).astype(jnp.int32)

    def body(i, prefix):
        cand = prefix | lax.shift_left(jnp.int32(1), 30 - i)
        return jnp.where(count_ge(cand) >= kf, cand, prefix)

    return lax.fori_loop(0, 31, body, prefix)


def _kth_largest_wide(count_ge, k, digit_bits=4):
    kf = jnp.float32(k)
    prefix = jnp.where(count_ge(jnp.zeros((1, 1), jnp.int32)) >= kf, 0, INT32_MIN).astype(jnp.int32)
    hi = 31
    while hi > 0:
        lo = max(hi - digit_bits, 0)
        digit = jnp.zeros((1, 1), jnp.int32)
        for d in range(1, 1 << (hi - lo)):
            digit = digit + jnp.where(count_ge(prefix | (d << lo)) >= kf, 1, 0)
        prefix = prefix | (digit * (1 << lo))
        hi = lo
    return prefix


def _odd_columns(d_model):
    n_q = d_model
    n_kv = ATT_KV_HEADS * ATT_HD
    n_qi = IDX_HEADS * IDX_DIM
    off = dict(q=0, qi=n_q, k=n_q + n_qi, v=n_q + n_qi + n_kv, ki=n_q + n_qi + 2 * n_kv)
    off["wi"] = off["ki"] + IDX_DIM
    off["end"] = off["wi"] + LANES
    return off


def _odd_weight(w_in_odd, d_model):
    n_kv = ATT_KV_HEADS * ATT_HD
    n_qi = IDX_HEADS * IDX_DIM
    cuts = [d_model, d_model + n_kv, d_model + 2 * n_kv, d_model + 2 * n_kv + n_qi, d_model + 2 * n_kv + n_qi + IDX_DIM]
    q, k, v, qi, ki, wi = jnp.split(w_in_odd, cuts, axis=1)
    pad = jnp.zeros((w_in_odd.shape[0], LANES - wi.shape[1]), w_in_odd.dtype)
    return jnp.concatenate([q, qi, k, v, ki, wi, pad], axis=1)


DSA_KEY_CLASSES = 8


def _dsa_kernel(q_ref, qi_ref, wi_ref, ki_ref, k_ref, v_ref, o_ref, kib, kb, vb, key_ref, *, topk, n_classes):
    i = pl.program_id(1)
    qb, n_keys = q_ref.shape[0], ki_ref.shape[0]
    group = q_ref.shape[1] // ATT_HD // ATT_KV_HEADS

    @pl.when(i == 0)
    def _():
        kib[...] = ki_ref[...].astype(BF16)
        kb[...] = k_ref[...].astype(BF16)
        vb[...] = v_ref[...].astype(BF16)

    def attend(nk):
        score = jnp.zeros((qb, nk), F32)
        for h in range(IDX_HEADS):
            s = _bdot_nt(qi_ref[:, h * IDX_DIM:(h + 1) * IDX_DIM], kib[0:nk, :]) * (IDX_DIM ** -0.5)
            score = score + jnp.maximum(s, 0.0) * wi_ref[:, h:h + 1]
        score = score * (IDX_HEADS ** -0.5)
        q_pos = i * qb + lax.broadcasted_iota(jnp.int32, (qb, nk), 0)
        causal = lax.broadcasted_iota(jnp.int32, (qb, nk), 1) <= q_pos
        key_ref[:, 0:nk] = _sortable_keys(jnp.where(causal, score, NEG_INF))

        def count_ge(t):
            return jnp.sum(jnp.where(key_ref[:, 0:nk] >= t, 1.0, 0.0), axis=-1, keepdims=True)

        thr = _kth_largest(count_ge, (qb, 1), topk)
        bias = jnp.where(causal, jnp.where(key_ref[:, 0:nk] >= thr, 0.0, NEG_INF), NEG_INF)
        bias = jnp.concatenate([bias] * group, axis=0)
        for n in range(ATT_KV_HEADS):
            heads = range(n * group, (n + 1) * group)
            qs = jnp.concatenate([q_ref[:, h * ATT_HD:(h + 1) * ATT_HD] for h in heads], axis=0)
            s = _bdot_nt(qs, kb[0:nk, n * ATT_HD:(n + 1) * ATT_HD]) * (ATT_HD ** -0.5) + bias
            e = jnp.exp(s - jnp.max(s, axis=-1, keepdims=True))
            o = _bdot(e, vb[0:nk, n * ATT_HD:(n + 1) * ATT_HD]) / jnp.sum(e, axis=-1, keepdims=True)
            for g, h in enumerate(heads):
                o_ref[:, h * ATT_HD:(h + 1) * ATT_HD] = o[g * qb:(g + 1) * qb, :].astype(BF16)

    blocks_per_class = n_keys // qb // n_classes
    for c in range(n_classes):
        pl.when(i // blocks_per_class == c)(functools.partial(attend, (c + 1) * blocks_per_class * qb))


PAGES_PER_CHUNK = 8
GROUP_LANES = LANES // PAGES_PER_CHUNK
IDX_RING = 4
KV_RING = 8


def _group_sum(x):
    sh = 1
    while sh < GROUP_LANES:
        x = x + pltpu.roll(x, sh, axis=1)
        sh *= 2
    return x


def _group_spread(x):
    sh = 1
    while sh < GROUP_LANES:
        x = jnp.maximum(x, pltpu.roll(x, LANES - sh, axis=1))
        sh *= 2
    return x


def _across_groups(x, op):
    sh = GROUP_LANES
    while sh < LANES:
        x = op(x, pltpu.roll(x, sh, axis=1))
        sh *= 2
    return x


def _dsa_step_kernel(pt_ref, qit_ref, wq_ref, wl_ref, kin_ref, kn_ref, vn_ref, ex_ref, ci_hbm, ck_hbm, cv_hbm, o_ref,
                     ibuf, kvbuf, isem, kvsem, qit_s, wq_s, key_ref, att_ref, *, topk):
    b = pl.program_id(0)
    nb = pl.num_programs(0)
    npc = PAGES_PER_CHUNK
    n_chunks = key_ref.shape[0]
    assert GROUP_LANES == IDX_HEADS

    def page_copy(src, buf, sem, seq, chunk, slot, pg):
        return pltpu.make_async_copy(src.at[pt_ref[seq, chunk * npc + pg]], buf.at[slot, pg], sem.at[slot])

    def start(src, buf, sem, seq, chunk, slot):
        for pg in range(npc):
            page_copy(src, buf, sem, seq, chunk, slot, pg).start()

    def wait(src, buf, sem, slot):
        for pg in range(npc):
            page_copy(src, buf, sem, 0, 0, slot, pg).wait()

    ri, rk = ibuf.shape[0], kvbuf.shape[0]
    kv_jobs = 2 * n_chunks

    def idx_issue(job):
        @pl.when(job < nb * n_chunks)
        def _():
            start(ci_hbm, ibuf, isem, job // n_chunks, job % n_chunks, job % ri)

    def kv_issue(job):
        seq, jj, slot = job // kv_jobs, job % kv_jobs, job % rk

        @pl.when(jnp.logical_and(job < nb * kv_jobs, jj < n_chunks))
        def _():
            start(ck_hbm, kvbuf, kvsem, seq, jj, slot)

        @pl.when(jnp.logical_and(job < nb * kv_jobs, jj >= n_chunks))
        def _():
            start(cv_hbm, kvbuf, kvsem, seq, jj - n_chunks, slot)

    @pl.when(b == 0)
    def _():
        for job in range(ri - 1):
            idx_issue(jnp.int32(job))
        for job in range(rk - 1):
            kv_issue(jnp.int32(job))

    lane = lax.broadcasted_iota(jnp.int32, (1, LANES), 1)
    lane_grp = lax.shift_right_logical(lane, int(math.log2(GROUP_LANES)))
    last_in_grp = (lane & (GROUP_LANES - 1)) == GROUP_LANES - 1
    for pg in range(npc):
        qit_s[pg] = jnp.where(lane_grp == pg, qit_ref[...], 0.0).astype(BF16)
        wq_s[pg] = jnp.where(lane_grp == pg, wq_ref[...], 0.0).astype(BF16)
    w_lane = wl_ref[...]
    kv_rows = kvbuf.shape[2]
    q_per_kv = GROUP_LANES // ATT_KV_HEADS
    kv_of_lane = (lane & (GROUP_LANES - 1)) // q_per_kv
    pairs = lambda n: (lax.broadcasted_iota(jnp.int32, (n, LANES), 0) & (ATT_KV_HEADS - 1)) == kv_of_lane
    first_rows = lambda n: lax.broadcasted_iota(jnp.int32, (n, LANES), 0) < ATT_KV_HEADS

    def idx_score(raw):
        s = jnp.maximum(raw * (IDX_DIM ** -0.5), 0.0) * w_lane
        return _group_sum(s) * (IDX_HEADS ** -0.5)

    def idx_chunk(c, carry):
        job = b * n_chunks + c
        slot = job % ri
        idx_issue(job + ri - 1)
        wait(ci_hbm, ibuf, isem, slot)
        acc = jnp.zeros((PAGE_SIZE, LANES), F32)
        for pg in range(npc):
            acc = acc + jnp.dot(ibuf[slot, pg].astype(BF16), qit_s[pg], preferred_element_type=F32)
        key_ref[c] = _sortable_keys(jnp.where(last_in_grp, idx_score(acc), NEG_INF))
        return carry

    lax.fori_loop(0, n_chunks, idx_chunk, 0)
    rows8 = lambda r: jnp.broadcast_to(r, (8, r.shape[1]))
    new_raw = _bdot(rows8(kin_ref[...]), qit_ref[...])[0:1, :]
    key_new = _sortable_keys(jnp.where(lane == GROUP_LANES - 1, idx_score(new_raw), NEG_INF))

    assert n_chunks <= GROUP_LANES
    dense = key_ref[0]
    for c in range(1, n_chunks):
        dense = jnp.maximum(dense, pltpu.roll(key_ref[c], LANES - c, axis=1))

    def count_ge(t):
        tot = jnp.sum(jnp.where(dense >= t, 1.0, 0.0), axis=0, keepdims=True) + jnp.where(key_new >= t, 1.0, 0.0)
        return jnp.sum(tot, axis=1, keepdims=True)

    thr = _kth_largest_wide(count_ge, topk)
    new_sel = jnp.max(jnp.where(key_new >= thr, 1.0, 0.0), axis=1, keepdims=True) > 0.0

    def k_chunk(c, carry):
        job = b * kv_jobs + c
        slot = job % rk
        kv_issue(job + rk - 1)
        wait(ck_hbm, kvbuf, kvsem, slot)
        acc = jnp.zeros((kv_rows, LANES), F32)
        for pg in range(npc):
            acc = acc + jnp.dot(kvbuf[slot, pg].astype(BF16), wq_s[pg], preferred_element_type=F32)
        sel = _group_spread(jnp.where(key_ref[c] >= thr, 1.0, 0.0))
        sel_rows = jnp.dot(ex_ref[...], sel.astype(BF16), preferred_element_type=F32)
        att_ref[c] = jnp.where(pairs(kv_rows), jnp.where(sel_rows > 0.0, acc * (ATT_HD ** -0.5), NEG_INF), NEG_INF)
        return carry

    lax.fori_loop(0, n_chunks, k_chunk, 0)
    new_pairs = jnp.where(first_rows(8), jnp.where(pairs(8), 1.0, 0.0), 0.0)
    s_new = jnp.sum(new_pairs * _bdot(kn_ref[...], wq_ref[...]), axis=0, keepdims=True)
    s_new = jnp.where(new_sel, s_new * (ATT_HD ** -0.5), NEG_INF)
    m = jnp.max(jnp.max(att_ref[...], axis=0), axis=0, keepdims=True)
    m = jnp.maximum(_across_groups(m, jnp.maximum), s_new)
    e_new = jnp.where(lane < GROUP_LANES, jnp.exp(s_new - m), 0.0)

    def v_chunk(c, carry):
        acc, lsum = carry
        job = b * kv_jobs + n_chunks + c
        slot = job % rk
        kv_issue(job + rk - 1)
        wait(cv_hbm, kvbuf, kvsem, slot)
        et = jnp.exp(att_ref[c] - m).T
        lsum = lsum + jnp.sum(et, axis=1, keepdims=True)
        for pg in range(npc):
            acc = acc + _bdot(et[pg * GROUP_LANES:(pg + 1) * GROUP_LANES, :], kvbuf[slot, pg])
        return acc, lsum

    acc, lsum = lax.fori_loop(0, n_chunks, v_chunk,
                              (jnp.zeros((GROUP_LANES, ATT_HD), F32), jnp.zeros((LANES, 1), F32)))
    top = lambda r: jnp.concatenate([r, jnp.zeros((LANES - r.shape[0], LANES), F32)], axis=0)
    et_new = top(new_pairs * e_new).T
    acc = acc + _bdot(et_new[0:GROUP_LANES, :], top(vn_ref[...]))
    lsum = lsum + jnp.sum(et_new, axis=1, keepdims=True)
    l16 = lsum[0:GROUP_LANES, :]
    for pg in range(1, npc):
        l16 = l16 + lsum[pg * GROUP_LANES:(pg + 1) * GROUP_LANES, :]
    o_ref[...] = (acc / l16).astype(BF16)


def _dsa_step(z2d, cache_k, cache_v, cache_kidx, page_table, *, d_model):
    off = _odd_columns(d_model)
    bsz = z2d.shape[0]
    n_pages = page_table.shape[1]
    n_phys, page, kvh, hd = cache_k.shape
    n_kv = kvh * hd
    heads = d_model // hd
    assert page == PAGE_SIZE and heads == GROUP_LANES and n_pages % PAGES_PER_CHUNK == 0
    n_chunks = n_pages // PAGES_PER_CHUNK
    topk = min(INDEX_TOPK, (n_pages * page + 1) // 4)
    q = z2d[:, off["q"]:off["q"] + d_model].reshape(bsz, kvh, heads // kvh, hd)
    qi = z2d[:, off["qi"]:off["qi"] + IDX_HEADS * IDX_DIM].reshape(bsz, IDX_HEADS, IDX_DIM)
    tile = lambda a: jnp.tile(a, (1, 1, PAGES_PER_CHUNK))
    qit = tile(jnp.transpose(qi, (0, 2, 1)))
    wq = tile(jnp.transpose(q.reshape(bsz, heads, hd), (0, 2, 1)))
    wl = tile(z2d[:, off["wi"]:off["wi"] + IDX_HEADS].reshape(bsz, 1, IDX_HEADS))
    kin = z2d[:, off["ki"]:off["ki"] + IDX_DIM].reshape(bsz, 1, IDX_DIM)
    new_rows = lambda c: jnp.pad(z2d[:, c:c + n_kv].reshape(bsz, kvh, hd), ((0, 0), (0, 8 - kvh), (0, 0)))
    kn, vn = new_rows(off["k"]), new_rows(off["v"])
    expand = jnp.repeat(jnp.eye(page, dtype=BF16), kvh, axis=0)
    per_seq = lambda a: pl.BlockSpec((None,) + a.shape[1:], lambda b, pt: (b, 0, 0))
    hbm = pl.BlockSpec(memory_space=pl.ANY)
    grid_spec = pltpu.PrefetchScalarGridSpec(
        num_scalar_prefetch=1,
        grid=(bsz,),
        in_specs=[per_seq(qit), per_seq(wq), per_seq(wl), per_seq(kin), per_seq(kn), per_seq(vn),
                  pl.BlockSpec(expand.shape, lambda b, pt: (0, 0)), hbm, hbm, hbm],
        out_specs=pl.BlockSpec((None, heads, hd), lambda b, pt: (b, 0, 0)),
        scratch_shapes=[
            pltpu.VMEM((IDX_RING, PAGES_PER_CHUNK, page, IDX_DIM), F32),
            pltpu.VMEM((KV_RING, PAGES_PER_CHUNK, page * kvh, hd), F32),
            pltpu.SemaphoreType.DMA((IDX_RING,)),
            pltpu.SemaphoreType.DMA((KV_RING,)),
            pltpu.VMEM((PAGES_PER_CHUNK, IDX_DIM, LANES), BF16),
            pltpu.VMEM((PAGES_PER_CHUNK, hd, LANES), BF16),
            pltpu.VMEM((n_chunks, page, LANES), jnp.int32),
            pltpu.VMEM((n_chunks, page * kvh, LANES), F32),
        ])
    o = pl.pallas_call(
        functools.partial(_dsa_step_kernel, topk=topk),
        grid_spec=grid_spec,
        out_shape=jax.ShapeDtypeStruct((bsz, heads, hd), BF16),
        compiler_params=_params(("arbitrary",), 40),
        name="dsa_step",
    )(page_table, qit, wq, wl, kin, kn, vn, expand, cache_kidx, cache_k.reshape(n_phys, page * kvh, hd),
      cache_v.reshape(n_phys, page * kvh, hd))
    return o.reshape(bsz, d_model)


def _dsa_prompt(z2d, *, batch, seq, d_model, qb):
    off = _odd_columns(d_model)
    nq = seq // qb
    n_kv = ATT_KV_HEADS * ATT_HD
    topk = min(INDEX_TOPK, seq // 4)
    rows = lambda w, c: pl.BlockSpec((qb, w), lambda b, i: (b * nq + i, c // w))
    whole = lambda w, c: pl.BlockSpec((seq, w), lambda b, i: (b, c // w))
    n_classes = math.gcd(nq, DSA_KEY_CLASSES)
    return pl.pallas_call(
        functools.partial(_dsa_kernel, topk=topk, n_classes=n_classes),
        grid=(batch, nq),
        in_specs=[rows(d_model, off["q"]), rows(IDX_HEADS * IDX_DIM, off["qi"]), rows(LANES, off["wi"]),
                  whole(IDX_DIM, off["ki"]), whole(n_kv, off["k"]), whole(n_kv, off["v"])],
        out_specs=pl.BlockSpec((qb, d_model), lambda b, i: (b * nq + i, 0)),
        out_shape=jax.ShapeDtypeStruct((batch * seq, d_model), BF16),
        scratch_shapes=[pltpu.VMEM((seq, IDX_DIM), BF16), pltpu.VMEM((seq, n_kv), BF16), pltpu.VMEM((seq, n_kv), BF16),
                        pltpu.VMEM((qb, seq), jnp.int32)],
        compiler_params=_params(("parallel", "arbitrary"), 56),
        name="dsa_prompt",
    )(z2d, z2d, z2d, z2d, z2d, z2d)


def _trunk(x2d, mods, wts, cfg, even_fn, odd_fn):
    tm, tf, tpb = cfg["tm"], cfg["tf"], cfg["tpb"]
    ffn = functools.partial(_ffn, tm=tm, tf=tf, tiles_per_batch=tpb, vmem_mib=cfg["ffn_vmem"])
    x = x2d
    states = []
    depth = wts["norm_g"].shape[0]
    for layer in range(depth):
        g, mod = wts["norm_g"][layer], mods[layer]
        x = ffn(x, g[0], mod, 0, wts["w_ffn_in"], wts["w_ffn_out"], (layer, 0), wts["final_g"], final_norm=False)
        x, st = (even_fn if layer % 2 == 0 else odd_fn)(x, g[1], mod)
        states.append(st)
        x = ffn(x, g[2], mod, 2, wts["w_ffn_in"], wts["w_ffn_out"], (layer, 1), wts["final_g"],
                final_norm=layer == depth - 1)
    return x, states


def kernel(x_prompt, x_sample, c_prompt, c_sample, state_ret, state_s5_re, state_s5_im, cache_k, cache_v, cache_kidx,
           page_table, norm_g, w_ada, b_ada, w_ffn_in, w_ffn_out, w_in_even, w_out_even, s5_a_re, s5_a_im, s5_log_dt,
           s5_b_re, s5_b_im, s5_c_re, s5_c_im, s5_d, w_glu, b_glu, w_in_odd, w_out_odd, final_g):
    bp, lp, d = x_prompt.shape
    bs, ls, _ = x_sample.shape
    assert ls == 1
    depth = w_ada.shape[0]
    n_mod = N_SUBLAYERS * 3

    rows = -(-(bp + bs) // 8) * 8
    c_all = jnp.concatenate([c_prompt, c_sample, jnp.zeros((rows - bp - bs, d), F32)], axis=0)
    mod_all = _ada_mod(c_all, w_ada, b_ada)
    mods_p = [mod_all[l, :bp].reshape(bp, n_mod, 1, d) for l in range(depth)]
    mods_s = [jnp.transpose(mod_all[l, bp:bp + bs].reshape(bs, n_mod, d), (1, 0, 2))[None] for l in range(depth)]

    p = dict(w_in_even=w_in_even, w_out_even=w_out_even, s5_a_re=s5_a_re, s5_a_im=s5_a_im, s5_log_dt=s5_log_dt,
             s5_b_re=s5_b_re, s5_b_im=s5_b_im, s5_c_re=s5_c_re, s5_c_im=s5_c_im, s5_d=s5_d, w_glu=w_glu, b_glu=b_glu)
    sp = _s5_prepare(p)
    w_odd = _odd_weight(w_in_odd, d)
    off = _odd_columns(d)
    n_kv = ATT_KV_HEADS * ATT_HD
    wts = dict(norm_g=norm_g, w_ffn_in=w_ffn_in, w_ffn_out=w_ffn_out, final_g=final_g)
    odd_tn = 768

    def cache_parts(z, b, l):
        return (z[:, off["k"]:off["k"] + n_kv].reshape(b, l, ATT_KV_HEADS, ATT_HD),
                z[:, off["v"]:off["v"] + n_kv].reshape(b, l, ATT_KV_HEADS, ATT_HD),
                z[:, off["ki"]:off["ki"] + IDX_DIM].reshape(b, l, IDX_DIM))

    cfg_p = dict(tm=min(1024, lp), tf=256, ffn_vmem=60)
    cfg_p["tpb"] = lp // cfg_p["tm"]

    def even_p(x, g, mod):
        zero_s5 = jnp.zeros((bp,) + state_s5_re.shape[1:], F32)
        x, s_ret, sr, si = _even_layer(x, g, mod, p, jnp.zeros((bp,) + state_ret.shape[1:], F32), zero_s5, zero_s5,
                                       batch=bp, seq=lp, pos0=0, tm=cfg_p["tm"], tiles_per_batch=cfg_p["tpb"],
                                       rb=min(256, lp), chunk=min(RET_CHUNK, lp), s5_t=min(128, lp), sp=sp)
        return x, (s_ret, sr, si)

    def odd_p(x, g, mod):
        z = _proj(x, g, mod, 1, w_odd, tm=cfg_p["tm"], tn=odd_tn, tiles_per_batch=cfg_p["tpb"], vmem_mib=52)
        o = _dsa_prompt(z, batch=bp, seq=lp, d_model=d, qb=min(128, lp))
        x = _outproj([(o, d, None)], w_out_odd, x, mod, tm=cfg_p["tm"], tn=1024, tiles_per_batch=cfg_p["tpb"],
                     vmem_mib=48)
        return x, cache_parts(z, bp, lp)

    y_p, (ev_p, od_p) = _trunk(x_prompt.reshape(bp * lp, d), mods_p, wts, cfg_p, even_p, odd_p)

    past_len = page_table.shape[1] * cache_k.shape[1]
    cfg_s = dict(tm=bs, tf=512, tpb=1, ffn_vmem=40)

    def even_s(x, g, mod):
        x, s_ret, sr, si = _even_layer(x, g, mod, p, state_ret, state_s5_re, state_s5_im, batch=bs, seq=1,
                                       pos0=past_len, tm=bs, tiles_per_batch=1, rb=1, chunk=1, s5_t=1, sp=sp)
        return x, (s_ret, sr, si)

    def odd_s(x, g, mod):
        z = _proj(x, g, mod, 1, w_odd, tm=bs, tn=odd_tn, tiles_per_batch=1, vmem_mib=40)
        o = _dsa_step(z, cache_k, cache_v, cache_kidx, page_table, d_model=d)
        x = _outproj([(o, d, None)], w_out_odd, x, mod, tm=bs, tn=1024, tiles_per_batch=1, vmem_mib=40)
        return x, cache_parts(z, bs, 1)

    y_s, (ev_s, od_s) = _trunk(x_sample.reshape(bs, d), mods_s, wts, cfg_s, even_s, odd_s)

    return (y_p.reshape(bp, lp, d), y_s.reshape(bs, 1, d), *ev_p, *od_p, *ev_s, *od_s)
```

```python
import functools
import math

import jax
import jax.numpy as jnp
from jax import lax
from jax.experimental import pallas as pl
from jax.experimental.pallas import tpu as pltpu

F32 = jnp.float32
BF16 = jnp.bfloat16

EPS = 1e-6
FFN_RES = 0.5
N_SUBLAYERS = 3
RET_HEADS = 4
RET_CHUNK = 128
ROPE_BASE = 10000.0
S5_GROUP_CH = 16
S5_STATE = 64
ATT_HD = 128
ATT_KV_HEADS = 4
IDX_HEADS = 16
IDX_DIM = 128
INDEX_TOPK = 256
PAGE_SIZE = 128

LANES = 128
MIB = 1024 * 1024
NEG_INF = float("-inf")


def _params(sem, vmem_mib):
    return pltpu.CompilerParams(dimension_semantics=sem, vmem_limit_bytes=vmem_mib * MIB)


def _bdot(a, b):
    return jnp.dot(a.astype(BF16), b.astype(BF16), preferred_element_type=F32)


def _bdot_nt(a, b):
    return lax.dot_general(a.astype(BF16), b.astype(BF16), (((1,), (1,)), ((), ())),
                           preferred_element_type=F32)


def _bdot_tn(a, b):
    return lax.dot_general(a.astype(BF16), b.astype(BF16), (((0,), (0,)), ((), ())),
                           preferred_element_type=F32)


def _silu(x):
    return x * jax.nn.sigmoid(x)


def _rms(x, g):
    ms = jnp.mean(x * x, axis=-1, keepdims=True)
    return x * lax.rsqrt(ms + EPS) * g


def _row_chunks(n_rows, chunk, fn):
    if n_rows <= chunk:
        fn(slice(0, n_rows))
        return
    assert n_rows % chunk == 0

    def body(i, c):
        fn(pl.ds(pl.multiple_of(i * chunk, chunk), chunk))
        return c

    lax.fori_loop(0, n_rows // chunk, body, 0)


ROW_CHUNK = 256


def _norm_mod_to(h_ref, x_ref, g_ref, sc_ref, sh_ref):
    n_rows = x_ref.shape[0]
    assert sc_ref.shape[0] == 1 or n_rows <= ROW_CHUNK

    def fn(rows):
        y = _rms(x_ref[rows, :], g_ref[...])
        h_ref[rows, :] = (y * (1.0 + sc_ref[...]) + sh_ref[...]).astype(BF16)

    _row_chunks(n_rows, ROW_CHUNK, fn)


def _ada_kernel(c_ref, w_ref, b_ref, o_ref):
    o_ref[...] = _bdot(_silu(c_ref[...]), w_ref[...]) + b_ref[...]


def _ada_mod(c_all, w_ada, b_ada):
    depth, d, n = w_ada.shape
    rows = c_all.shape[0]
    tn = 1024
    return pl.pallas_call(
        _ada_kernel,
        grid=(depth, n // tn),
        in_specs=[
            pl.BlockSpec((rows, d), lambda l, j: (0, 0)),
            pl.BlockSpec((None, d, tn), lambda l, j: (l, 0, j)),
            pl.BlockSpec((None, 1, tn), lambda l, j: (l, 0, j)),
        ],
        out_specs=pl.BlockSpec((None, rows, tn), lambda l, j: (l, 0, j)),
        out_shape=jax.ShapeDtypeStruct((depth, rows, n), F32),
        compiler_params=_params(("arbitrary", "arbitrary"), 40),
        name="ada_mod",
    )(c_all, w_ada, b_ada.reshape(depth, 1, n))


def _mod_specs(mod, sub, tiles_per_batch, which):
    _, _, r, d = mod.shape
    specs = []
    for k in which:
        idx = sub * 3 + k
        specs.append(pl.BlockSpec((None, None, r, d),
                                  lambda i, j, idx=idx: (i // tiles_per_batch, idx, 0, 0)))
    return specs


def _ffn_kernel(x_ref, g_ref, sh_ref, sc_ref, gt_ref, wa_ref, wb_ref, wo_ref, fg_ref, o_ref, h_ref,
                *, final_norm):
    j = pl.program_id(1)
    nj = pl.num_programs(1)

    @pl.when(j == 0)
    def _():
        _norm_mod_to(h_ref, x_ref, g_ref, sc_ref, sh_ref)
        o_ref[...] = jnp.zeros(o_ref.shape, F32)

    h = h_ref[...]
    a = jnp.dot(h, wa_ref[...].astype(BF16), preferred_element_type=F32)
    b = jnp.dot(h, wb_ref[...].astype(BF16), preferred_element_type=F32)
    o_ref[...] += _bdot(_silu(a) * b, wo_ref[...])

    @pl.when(j == nj - 1)
    def _():
        def fn(rows):
            y = x_ref[rows, :] + (FFN_RES * gt_ref[...]) * o_ref[rows, :]
            if final_norm:
                y = _rms(y, fg_ref[...])
            o_ref[rows, :] = y

        _row_chunks(x_ref.shape[0], ROW_CHUNK, fn)


def _ffn(x2d, norm_g, mod, sub, w_in, w_out, wsel, final_g, *, tm, tf, tiles_per_batch, final_norm, vmem_mib):
    m, d = x2d.shape
    f = w_out.shape[2]
    nf = f // tf
    assert m % tm == 0 and f % tf == 0
    row = lambda i, j: (0, 0)
    l, k = wsel
    return pl.pallas_call(
        functools.partial(_ffn_kernel, final_norm=final_norm),
        grid=(m // tm, nf),
        in_specs=[
            pl.BlockSpec((tm, d), lambda i, j: (i, 0)),
            pl.BlockSpec((1, d), row),
            *_mod_specs(mod, sub, tiles_per_batch, (0, 1, 2)),
            pl.BlockSpec((None, None, d, tf), lambda i, j: (l, k, 0, j)),
            pl.BlockSpec((None, None, d, tf), lambda i, j: (l, k, 0, nf + j)),
            pl.BlockSpec((None, None, tf, d), lambda i, j: (l, k, j, 0)),
            pl.BlockSpec((1, d), row),
        ],
        out_specs=pl.BlockSpec((tm, d), lambda i, j: (i, 0)),
        out_shape=jax.ShapeDtypeStruct((m, d), F32),
        scratch_shapes=[pltpu.VMEM((tm, d), BF16)],
        compiler_params=_params(("parallel", "arbitrary"), vmem_mib),
        name="ffn",
    )(x2d, norm_g.reshape(1, d), mod, mod, mod, w_in, w_in, w_out, final_g.reshape(1, d))


def _proj_kernel(x_ref, g_ref, sh_ref, sc_ref, w_ref, o_ref, h_ref):
    @pl.when(pl.program_id(1) == 0)
    def _():
        _norm_mod_to(h_ref, x_ref, g_ref, sc_ref, sh_ref)

    o_ref[...] = jnp.dot(h_ref[...], w_ref[...].astype(BF16), preferred_element_type=F32)


def _proj(x2d, norm_g, mod, sub, w, *, tm, tn, tiles_per_batch, vmem_mib):
    m, d = x2d.shape
    n = w.shape[1]
    assert m % tm == 0 and n % tn == 0
    return pl.pallas_call(
        _proj_kernel,
        grid=(m // tm, n // tn),
        in_specs=[
            pl.BlockSpec((tm, d), lambda i, j: (i, 0)),
            pl.BlockSpec((1, d), lambda i, j: (0, 0)),
            *_mod_specs(mod, sub, tiles_per_batch, (0, 1)),
            pl.BlockSpec((d, tn), lambda i, j: (0, j)),
        ],
        out_specs=pl.BlockSpec((tm, tn), lambda i, j: (i, j)),
        out_shape=jax.ShapeDtypeStruct((m, n), F32),
        scratch_shapes=[pltpu.VMEM((tm, d), BF16)],
        compiler_params=_params(("parallel", "arbitrary"), vmem_mib),
        name="mixer_in_proj",
    )(x2d, norm_g.reshape(1, d), mod, mod, w)


def _outproj_kernel(*refs, n_parts):
    a_refs = refs[:n_parts]
    w_ref, x_ref, gt_ref, o_ref = refs[n_parts:]
    acc = None
    k0 = 0
    for a_ref in a_refs:
        kk = a_ref.shape[1]
        p = jnp.dot(a_ref[...], w_ref[k0:k0 + kk, :].astype(BF16), preferred_element_type=F32)
        acc = p if acc is None else acc + p
        k0 += kk
    o_ref[...] = x_ref[...] + gt_ref[...] * acc


def _outproj(a_parts, w, x2d, mod, *, tm, tn, tiles_per_batch, vmem_mib):
    m, d = x2d.shape
    k = w.shape[0]
    assert sum(kw for _, kw, _ in a_parts) == k and m % tm == 0 and d % tn == 0
    _, _, r, _ = mod.shape
    return pl.pallas_call(
        functools.partial(_outproj_kernel, n_parts=len(a_parts)),
        grid=(m // tm, d // tn),
        in_specs=[
            *[pl.BlockSpec((tm, kw), imap or (lambda i, j: (i, 0))) for _, kw, imap in a_parts],
            pl.BlockSpec((k, tn), lambda i, j: (0, j)),
            pl.BlockSpec((tm, tn), lambda i, j: (i, j)),
            pl.BlockSpec((None, None, r, tn), lambda i, j: (i // tiles_per_batch, 5, 0, j)),
        ],
        out_specs=pl.BlockSpec((tm, tn), lambda i, j: (i, j)),
        out_shape=jax.ShapeDtypeStruct((m, d), F32),
        compiler_params=_params(("parallel", "arbitrary"), vmem_mib),
        name="mixer_out_proj",
    )(*[a for a, _, _ in a_parts], w, x2d, mod)


def _rot(x1, x2, cos, sin):
    return jnp.concatenate([x1 * cos - x2 * sin, x1 * sin + x2 * cos], axis=-1)


def _head_norm(o):
    mu = jnp.mean(o, axis=-1, keepdims=True)
    var = jnp.mean(jnp.square(o - mu), axis=-1, keepdims=True)
    return (o - mu) * lax.rsqrt(var + 1e-5)


def _ret_log_g(h):
    return math.log1p(-(2.0 ** (-5.0 - h)))


def _ret_kernel(q_ref, k_ref, v_ref, g_ref, cos_ref, sin_ref, s0_ref, o_ref, s_ref, *, chunk):
    c = pl.program_id(1)
    rb = q_ref.shape[0]
    dk = q_ref.shape[1] // RET_HEADS
    half = dk // 2

    @pl.when(c == 0)
    def _():
        s_ref[...] = s0_ref[...]

    ri = lax.broadcasted_iota(jnp.int32, (chunk, chunk), 0).astype(F32)
    ci = lax.broadcasted_iota(jnp.int32, (chunk, chunk), 1).astype(F32)
    diff = ri - ci
    rowf = lax.broadcasted_iota(jnp.int32, (chunk, dk), 0).astype(F32)
    for h in range(RET_HEADS):
        lg = _ret_log_g(h)
        dmat = jnp.where(diff >= 0, jnp.exp(jnp.maximum(diff, 0.0) * lg), 0.0)
        q_dec = jnp.exp((rowf + 1.0) * lg)
        k_dec = jnp.exp((chunk - 1.0 - rowf) * lg)
        c_dec = math.exp(chunk * lg)
        c0 = h * dk
        for ck in range(rb // chunk):
            rows = slice(ck * chunk, (ck + 1) * chunk)
            cos = cos_ref[rows, :]
            sin = sin_ref[rows, :]
            qr = _rot(q_ref[rows, c0:c0 + half], q_ref[rows, c0 + half:c0 + dk], cos, sin)
            kr = _rot(k_ref[rows, c0:c0 + half], k_ref[rows, c0 + half:c0 + dk], cos, sin) * (dk ** -0.5)
            vb = v_ref[rows, c0:c0 + dk].astype(BF16)
            qb = qr.astype(BF16)
            a = _bdot_nt(qb, kr) * dmat
            s = s_ref[h]
            o = _bdot(a, vb) + _bdot(qb, s) * q_dec
            s_ref[h] = s * c_dec + _bdot_tn(kr * k_dec, vb)
            o_ref[rows, c0:c0 + dk] = (_head_norm(o) * _silu(g_ref[rows, c0:c0 + dk])).astype(BF16)


def _retention(z2d, col0, cos, sin, state0, *, batch, seq, rb, chunk):
    _, h, dk, dv = state0.shape
    w = h * dk
    nb = seq // rb
    assert seq % rb == 0 and rb % chunk == 0 and col0 % w == 0
    cb = col0 // w
    zspec = lambda k: pl.BlockSpec((rb, w), lambda b, c, k=k: (b * nb + c, cb + k))
    tab = pl.BlockSpec((rb, dk // 2), lambda b, c: (c, 0))
    st = pl.BlockSpec((None, h, dk, dv), lambda b, c: (b, 0, 0, 0))
    return pl.pallas_call(
        functools.partial(_ret_kernel, chunk=chunk),
        grid=(batch, nb),
        in_specs=[zspec(0), zspec(1), zspec(2), zspec(3), tab, tab, st],
        out_specs=[pl.BlockSpec((rb, w), lambda b, c: (b * nb + c, 0)), st],
        out_shape=[jax.ShapeDtypeStruct((batch * seq, w), BF16),
                   jax.ShapeDtypeStruct(state0.shape, F32)],
        compiler_params=_params(("parallel", "arbitrary"), 40),
        name="retention",
    )(z2d, z2d, z2d, z2d, cos, sin, state0)


def _rope_tables(pos, half):
    freqs = ROPE_BASE ** (-jnp.arange(half, dtype=F32) / half)
    ang = pos.astype(F32)[:, None] * freqs[None, :]
    return jnp.cos(ang), jnp.sin(ang)


def _r16(x):
    return x.astype(BF16).astype(F32)


def _ret_step_kernel(q_ref, krow_ref, kcol_ref, v_ref, g_ref, cos_ref, sin_ref, cosc_ref, sinc_ref, s0_ref,
                     o_ref, s_ref):
    dk = q_ref.shape[1] // RET_HEADS
    half = dk // 2
    cos, sin = cos_ref[...], sin_ref[...]
    cosc, sinc = cosc_ref[...], sinc_ref[...]
    for h in range(RET_HEADS):
        dec = math.exp(_ret_log_g(h))
        c0 = h * dk
        qr = _r16(_rot(q_ref[:, c0:c0 + half], q_ref[:, c0 + half:c0 + dk], cos, sin))
        kr = _r16(_rot(krow_ref[:, c0:c0 + half], krow_ref[:, c0 + half:c0 + dk], cos, sin) * (dk ** -0.5))
        k1, k2 = kcol_ref[h, 0:half, :], kcol_ref[h, half:dk, :]
        kc = _r16(jnp.concatenate([k1 * cosc - k2 * sinc, k1 * sinc + k2 * cosc], axis=0) * (dk ** -0.5))
        v = _r16(v_ref[:, c0:c0 + dk])
        a = jnp.sum(qr * kr, axis=-1, keepdims=True)
        s = s0_ref[h]
        o = _r16(a) * v + _bdot(qr, s) * dec
        s_ref[h] = s * dec + kc * v
        o_ref[:, c0:c0 + dk] = (_head_norm(o) * _silu(g_ref[:, c0:c0 + dk])).astype(BF16)


def _retention_step(z2d, cos, sin, state0):
    b, h, dk, dv = state0.shape
    w = h * dk
    z3 = z2d.reshape(b, 1, z2d.shape[1])
    kcol = z2d[:, w:2 * w].reshape(b, h, dk, 1)
    half = dk // 2
    zspec = lambda k: pl.BlockSpec((None, 1, w), lambda i, k=k: (i, 0, k))
    row = pl.BlockSpec((1, half), lambda i: (0, 0))
    col = pl.BlockSpec((half, 1), lambda i: (0, 0))
    st = pl.BlockSpec((None, h, dk, dv), lambda i: (i, 0, 0, 0))
    og, s = pl.pallas_call(
        _ret_step_kernel,
        grid=(b,),
        in_specs=[zspec(0), zspec(1), pl.BlockSpec((None, h, dk, 1), lambda i: (i, 0, 0, 0)), zspec(2), zspec(3),
                  row, row, col, col, st],
        out_specs=[pl.BlockSpec((None, 1, w), lambda i: (i, 0, 0)), st],
        out_shape=[jax.ShapeDtypeStruct((b, 1, w), BF16), jax.ShapeDtypeStruct(state0.shape, F32)],
        compiler_params=_params(("parallel",), 32),
        name="retention_step",
    )(z3, z3, kcol, z3, z3, cos, sin, cos.reshape(half, 1), sin.reshape(half, 1), state0)
    return og.reshape(b, w), s


S5_BLOCKS = 8


def _s5_disc(ar, ai, dt):
    mag = jnp.exp(dt * ar)
    abr = mag * jnp.cos(dt * ai)
    abi = mag * jnp.sin(dt * ai)
    den = ar * ar + ai * ai
    xr = abr - 1.0
    return abr, abi, (xr * ar + abi * ai) / den, (abi * ar - xr * ai) / den


def _s5_param_kernel(ar_ref, ai_ref, ldt_ref, arr_ref, air_ref, br_ref, bi_ref, abr_ref, abi_ref, bbr_ref, bbi_ref):
    dt = jnp.exp(ldt_ref[...])
    abr_ref[...], abi_ref[...], _, _ = _s5_disc(ar_ref[...], ai_ref[...], dt)
    _, _, fr, fi = _s5_disc(arr_ref[...], air_ref[...], dt)
    br, bi = br_ref[...], bi_ref[...]
    bbr_ref[...] = fr * br - fi * bi
    bbi_ref[...] = fr * bi + fi * br


def _s5_prepare(p):
    g, st = p["s5_a_re"].shape
    ch = p["s5_d"].shape[1]
    rep = lambda a: jnp.repeat(a, ch, axis=1)
    abr, abi, bbr, bbi = pl.pallas_call(
        _s5_param_kernel,
        out_shape=[jax.ShapeDtypeStruct((g, st), F32)] * 2 + [jax.ShapeDtypeStruct((g, st * ch), F32)] * 2,
        name="s5_discretise",
    )(p["s5_a_re"], p["s5_a_im"], p["s5_log_dt"].reshape(g, 1), rep(p["s5_a_re"]), rep(p["s5_a_im"]),
      p["s5_b_re"].reshape(g, st * ch), p["s5_b_im"].reshape(g, st * ch))
    nb = S5_BLOCKS
    gl = g // nb
    eye = jnp.eye(gl, dtype=F32)
    w_in = lambda bb: jnp.einsum("bgpc,gh->bgchp", bb.reshape(nb, gl, st, ch), eye).reshape(nb, gl * ch, gl * st)
    w_out = lambda c: jnp.einsum("bgcp,gh->bgphc", c.reshape(nb, gl, ch, st), eye).reshape(nb, gl * st, gl * ch)
    return dict(
        abr=abr.reshape(1, g * st), abi=abi.reshape(1, g * st),
        wbr=w_in(bbr).astype(BF16), wbi=w_in(bbi).astype(BF16),
        wcr=w_out(p["s5_c_re"]).astype(BF16), wci=w_out(p["s5_c_im"]).astype(BF16),
        d=p["s5_d"].reshape(1, g * ch), w_glu=p["w_glu"], b_glu=p["b_glu"].reshape(1, -1))


S5_SCAN_LANES = 1024


def _s5_kernel(u_ref, x0r_ref, x0i_ref, abr_ref, abi_ref, wbr_ref, wbi_ref, wcr_ref, wci_ref, d_ref, wg_ref,
               bg_ref, y_ref, xr_ref, xi_ref, bur, bui, yb, wgb, *, ns, t):
    nblk = wbr_ref.shape[0]
    uw, sw = wbr_ref.shape[1], wbr_ref.shape[2]
    n_tiles = nblk * sw // LANES
    sub = 8

    @pl.when(pl.program_id(0) == 0)
    def _():
        xr_ref[...] = x0r_ref[...]
        xi_ref[...] = x0i_ref[...]
        wgb[...] = wg_ref[...].astype(BF16)

    tiles_per_blk = sw // LANES
    lane_tile = lambda j: slice(j * LANES, (j + 1) * LANES)
    for b in range(nblk):
        ub = u_ref[:, b * uw:(b + 1) * uw].astype(BF16)
        pr = jnp.dot(ub, wbr_ref[b], preferred_element_type=F32)
        pi = jnp.dot(ub, wbi_ref[b], preferred_element_type=F32)
        for k in range(tiles_per_blk):
            bur[b * tiles_per_blk + k] = pr[:, lane_tile(k)]
            bui[b * tiles_per_blk + k] = pi[:, lane_tile(k)]

    def affine(ar, ai, xr, xi, vr, vi):
        return ar * xr - ai * xi + vr, ar * xi + ai * xr + vi

    per_pass = S5_SCAN_LANES // LANES
    for j0 in range(0, n_tiles, per_pass):
        tiles = list(range(j0, min(j0 + per_pass, n_tiles)))
        if t == 1:
            for j in tiles:
                ar = jnp.broadcast_to(abr_ref[:, lane_tile(j)], (ns, LANES))
                ai = jnp.broadcast_to(abi_ref[:, lane_tile(j)], (ns, LANES))
                nr, ni = affine(ar, ai, xr_ref[:, lane_tile(j)], xi_ref[:, lane_tile(j)], bur[j], bui[j])
                bur[j], bui[j] = nr, ni
                xr_ref[:, lane_tile(j)], xi_ref[:, lane_tile(j)] = nr, ni
            continue
        assert 2 * ns == sub and t % 2 == 0
        ar = [jnp.broadcast_to(abr_ref[:, lane_tile(j)], (sub, LANES)) for j in tiles]
        ai = [jnp.broadcast_to(abi_ref[:, lane_tile(j)], (sub, LANES)) for j in tiles]
        top = lax.broadcasted_iota(jnp.int32, (sub, LANES), 0) < ns
        swap = lambda x: pltpu.roll(x, ns, axis=0)

        def step(i, carry, tiles=tiles, ar=ar, ai=ai):
            rows = pl.ds(pl.multiple_of(i * sub, sub), sub)
            out = []
            for k, j in enumerate(tiles):
                pr_, pi_ = carry[k]
                vr, vi = bur[j, rows, :], bui[j, rows, :]
                er, ei = affine(ar[k], ai[k], swap(pr_), swap(pi_), vr, vi)
                orr, oi = affine(ar[k], ai[k], swap(er), swap(ei), vr, vi)
                nr, ni = jnp.where(top, er, orr), jnp.where(top, ei, oi)
                bur[j, rows, :] = nr
                bui[j, rows, :] = ni
                out.append((nr, ni))
            return tuple(out)

        twice = lambda x: jnp.concatenate([x, x], axis=0)
        fin = lax.fori_loop(0, t // 2, step,
                            tuple((twice(xr_ref[:, lane_tile(j)]), twice(xi_ref[:, lane_tile(j)])) for j in tiles))
        for k, j in enumerate(tiles):
            xr_ref[:, lane_tile(j)] = fin[k][0][ns:, :]
            xi_ref[:, lane_tile(j)] = fin[k][1][ns:, :]

    blk_states = lambda ref, b: jnp.concatenate(
        [ref[b * tiles_per_blk + k] for k in range(tiles_per_blk)], axis=1).astype(BF16)
    for b in range(nblk):
        yb[:, b * uw:(b + 1) * uw] = (jnp.dot(blk_states(bur, b), wcr_ref[b], preferred_element_type=F32)
                                      - jnp.dot(blk_states(bui, b), wci_ref[b], preferred_element_type=F32))
    gl = jax.nn.gelu(yb[...] + d_ref[...] * u_ref[...])
    gate = jax.nn.sigmoid(jnp.dot(gl.astype(BF16), wgb[...], preferred_element_type=F32) + bg_ref[...])
    y_ref[...] = (gl * gate).astype(BF16)


def _s5(u2d, col_blk, sp, x0r, x0i, *, batch, t):
    w = sp["d"].shape[1]
    n_state = sp["abr"].shape[1]
    rows = batch * t
    assert u2d.shape[0] % rows == 0
    full = lambda a: pl.BlockSpec(a.shape, lambda c, nd=a.ndim: (0,) * nd)
    st_spec = pl.BlockSpec((batch, n_state), lambda c: (0, 0))
    consts = [sp["abr"], sp["abi"], sp["wbr"], sp["wbi"], sp["wcr"], sp["wci"], sp["d"], sp["w_glu"], sp["b_glu"]]
    return pl.pallas_call(
        functools.partial(_s5_kernel, ns=batch, t=t),
        grid=(u2d.shape[0] // rows,),
        in_specs=[pl.BlockSpec((rows, w), lambda c: (c, col_blk)), st_spec, st_spec] + [full(a) for a in consts],
        out_specs=[pl.BlockSpec((rows, w), lambda c: (c, 0)), st_spec, st_spec],
        out_shape=[jax.ShapeDtypeStruct((u2d.shape[0], w), BF16), jax.ShapeDtypeStruct((batch, n_state), F32),
                   jax.ShapeDtypeStruct((batch, n_state), F32)],
        scratch_shapes=[pltpu.VMEM((n_state // LANES, rows, LANES), F32), pltpu.VMEM((n_state // LANES, rows, LANES), F32),
                        pltpu.VMEM((rows, w), F32), pltpu.VMEM(sp["w_glu"].shape, BF16)],
        compiler_params=_params(("arbitrary",), 48),
        name="s5",
    )(u2d, x0r.reshape(batch, n_state), x0i.reshape(batch, n_state), *consts)


def _even_layer(x2d, norm_g, mod, p, ret0, s5r0, s5i0, *, batch, seq, pos0, tm, tiles_per_batch, rb, chunk, s5_t,
                sp=None):
    _, h, dk, _ = ret0.shape
    rw = h * dk
    z = _proj(x2d, norm_g, mod, 1, p["w_in_even"], tm=tm, tn=1024, tiles_per_batch=tiles_per_batch, vmem_mib=52)
    cos, sin = _rope_tables(pos0 + jnp.arange(seq), dk // 2)
    if seq > 1:
        og, s_ret = _retention(z, 0, cos, sin, ret0, batch=batch, seq=seq, rb=rb, chunk=chunk)
    else:
        og, s_ret = _retention_step(z, cos, sin, ret0)
    sp = _s5_prepare(p) if sp is None else sp
    sw = sp["d"].shape[1]
    if seq > 1:
        u2d = jnp.transpose(z.reshape(batch, seq, -1)[:, :, 4 * rw:4 * rw + sw], (1, 0, 2)).reshape(seq * batch, sw)
        y, sr, si = _s5(u2d, 0, sp, s5r0, s5i0, batch=batch, t=s5_t)
        y_part = (y.reshape(seq, batch * sw), sw, lambda i, j: (i % tiles_per_batch, i // tiles_per_batch))
    else:
        y, sr, si = _s5(z, (4 * rw) // sw, sp, s5r0, s5i0, batch=batch, t=1)
        y_part = (y, sw, None)
    x_new = _outproj([(og, rw, None), y_part], p["w_out_even"], x2d, mod, tm=tm, tn=1024,
                     tiles_per_batch=tiles_per_batch, vmem_mib=48)
    return x_new, s_ret, sr.reshape(s5r0.shape), si.reshape(s5i0.shape)


INT32_MIN = -2 ** 31


def _sortable_keys(score):
    score = jnp.where(score == 0.0, 0.0, score)
    bits = lax.bitcast_convert_type(score, jnp.int32)
    return bits ^ (lax.shift_right_arithmetic(bits, 31) & 0x7FFFFFFF)


def _kth_largest(count_ge, shape, k):
    kf = jnp.float32(k)
    prefix = jnp.where(count_ge(jnp.zeros(shape, jnp.int32)) >= kf, 0, INT32_MIN).astype(jnp.int32)

    def body(i, prefix):
        cand = prefix | lax.shift_left(jnp.int32(1), 30 - i)
        return jnp.where(count_ge(cand) >= kf, cand, prefix)

    return lax.fori_loop(0, 31, body, prefix)


def _kth_largest_wide(count_ge, k, digit_bits=4):
    kf = jnp.float32(k)
    prefix = jnp.where(count_ge(jnp.zeros((1, 1), jnp.int32)) >= kf, 0, INT32_MIN).astype(jnp.int32)
    hi = 31
    while hi > 0:
        lo = max(hi - digit_bits, 0)
        digit = jnp.zeros((1, 1), jnp.int32)
        for d in range(1, 1 << (hi - lo)):
            digit = digit + jnp.where(count_ge(prefix | (d << lo)) >= kf, 1, 0)
        prefix = prefix | (digit * (1 << lo))
        hi = lo
    return prefix


def _odd_columns(d_model):
    n_q = d_model
    n_kv = ATT_KV_HEADS * ATT_HD
    n_qi = IDX_HEADS * IDX_DIM
    off = dict(q=0, qi=n_q, k=n_q + n_qi, v=n_q + n_qi + n_kv, ki=n_q + n_qi + 2 * n_kv)
    off["wi"] = off["ki"] + IDX_DIM
    off["end"] = off["wi"] + LANES
    return off


def _odd_weight(w_in_odd, d_model):
    n_kv = ATT_KV_HEADS * ATT_HD
    n_qi = IDX_HEADS * IDX_DIM
    cuts = [d_model, d_model + n_kv, d_model + 2 * n_kv, d_model + 2 * n_kv + n_qi, d_model + 2 * n_kv + n_qi + IDX_DIM]
    q, k, v, qi, ki, wi = jnp.split(w_in_odd, cuts, axis=1)
    pad = jnp.zeros((w_in_odd.shape[0], LANES - wi.shape[1]), w_in_odd.dtype)
    return jnp.concatenate([q, qi, k, v, ki, wi, pad], axis=1)


DSA_EXTENTS = 4


def _dsa_kernel(q_ref, qi_ref, wi_ref, ki_ref, k_ref, v_ref, o_ref, kib, kb, vb, key_ref, cut_ref, *, topk, nk, q0):
    i = pl.program_id(1)
    qb = q_ref.shape[0]
    group = q_ref.shape[1] // ATT_HD // ATT_KV_HEADS

    @pl.when(i == 0)
    def _():
        kib[...] = ki_ref[0:nk, :].astype(BF16)
        kb[...] = k_ref[0:nk, :].astype(BF16)
        vb[...] = v_ref[0:nk, :].astype(BF16)

    score = jnp.zeros((qb, nk), F32)
    for h in range(IDX_HEADS):
        s = _bdot_nt(qi_ref[:, h * IDX_DIM:(h + 1) * IDX_DIM], kib[...]) * (IDX_DIM ** -0.5)
        score = score + jnp.maximum(s, 0.0) * wi_ref[:, h:h + 1]
    score = score * (IDX_HEADS ** -0.5)
    q_pos = (q0 + i) * qb + lax.broadcasted_iota(jnp.int32, (qb, nk), 0)
    col = lax.broadcasted_iota(jnp.int32, (qb, nk), 1)
    key_ref[...] = _sortable_keys(jnp.where(col <= q_pos, score, NEG_INF))

    count = lambda hit: jnp.sum(hit, axis=-1, keepdims=True)
    one = lambda cond: jnp.where(cond, 1.0, 0.0)
    thr = _kth_largest(lambda t: count(one(key_ref[...] >= t)), (qb, 1), topk)

    cut_ref[...] = jnp.full((qb, 1), nk, jnp.int32)

    @pl.when(jnp.max(count(one(key_ref[...] >= thr))) > topk)
    def _():
        need = topk - count(one(key_ref[...] > thr))

        def bit(j, cut):
            cand = cut + lax.shift_left(jnp.int32(1), nk.bit_length() - 1 - j)
            below = count(jnp.where(key_ref[...] == thr, one(col < cand), 0.0))
            return jnp.where(below < need, cand, cut)

        cut_ref[...] = lax.fori_loop(0, nk.bit_length(), bit, jnp.zeros((qb, 1), jnp.int32))

    kt = key_ref[...]
    keep = jnp.where(kt > thr, 0.0, jnp.where(kt == thr, jnp.where(col <= cut_ref[...], 0.0, NEG_INF), NEG_INF))
    bias = jnp.concatenate([jnp.where(col <= q_pos, keep, NEG_INF)] * group, axis=0)
    for n in range(ATT_KV_HEADS):
        heads = range(n * group, (n + 1) * group)
        qs = jnp.concatenate([q_ref[:, h * ATT_HD:(h + 1) * ATT_HD] for h in heads], axis=0)
        s = _bdot_nt(qs, kb[:, n * ATT_HD:(n + 1) * ATT_HD]) * (ATT_HD ** -0.5) + bias
        e = jnp.exp(s - jnp.max(s, axis=-1, keepdims=True))
        o = _bdot(e, vb[:, n * ATT_HD:(n + 1) * ATT_HD]) / jnp.sum(e, axis=-1, keepdims=True)
        for g, h in enumerate(heads):
            o_ref[:, h * ATT_HD:(h + 1) * ATT_HD] = o[g * qb:(g + 1) * qb, :].astype(BF16)


PAGES_PER_CHUNK = 8
GROUP_LANES = LANES // PAGES_PER_CHUNK
IDX_RING = 4
KV_RING = 8


def _group_sum(x):
    sh = 1
    while sh < GROUP_LANES:
        x = x + pltpu.roll(x, sh, axis=1)
        sh *= 2
    return x


def _group_spread(x):
    sh = 1
    while sh < GROUP_LANES:
        x = jnp.maximum(x, pltpu.roll(x, LANES - sh, axis=1))
        sh *= 2
    return x


def _across_groups(x, op):
    sh = GROUP_LANES
    while sh < LANES:
        x = op(x, pltpu.roll(x, sh, axis=1))
        sh *= 2
    return x


def _dsa_step_kernel(pt_ref, qit_ref, wq_ref, wl_ref, kin_ref, kn_ref, vn_ref, ex_ref, ci_hbm, ck_hbm, cv_hbm, o_ref,
                     ibuf, kvbuf, isem, kvsem, qit_s, wq_s, key_ref, att_ref, *, topk):
    b = pl.program_id(0)
    nb = pl.num_programs(0)
    npc = PAGES_PER_CHUNK
    n_chunks = key_ref.shape[0]
    assert GROUP_LANES == IDX_HEADS

    def page_copy(src, buf, sem, seq, chunk, slot, pg):
        return pltpu.make_async_copy(src.at[pt_ref[seq, chunk * npc + pg]], buf.at[slot, pg], sem.at[slot])

    def start(src, buf, sem, seq, chunk, slot):
        for pg in range(npc):
            page_copy(src, buf, sem, seq, chunk, slot, pg).start()

    def wait(src, buf, sem, slot):
        for pg in range(npc):
            page_copy(src, buf, sem, 0, 0, slot, pg).wait()

    ri, rk = ibuf.shape[0], kvbuf.shape[0]
    kv_jobs = 2 * n_chunks

    def idx_issue(job):
        @pl.when(job < nb * n_chunks)
        def _():
            start(ci_hbm, ibuf, isem, job // n_chunks, job % n_chunks, job % ri)

    def kv_issue(job):
        seq, jj, slot = job // kv_jobs, job % kv_jobs, job % rk

        @pl.when(jnp.logical_and(job < nb * kv_jobs, jj < n_chunks))
        def _():
            start(ck_hbm, kvbuf, kvsem, seq, jj, slot)

        @pl.when(jnp.logical_and(job < nb * kv_jobs, jj >= n_chunks))
        def _():
            start(cv_hbm, kvbuf, kvsem, seq, jj - n_chunks, slot)

    @pl.when(b == 0)
    def _():
        for job in range(ri - 1):
            idx_issue(jnp.int32(job))
        for job in range(rk - 1):
            kv_issue(jnp.int32(job))

    lane = lax.broadcasted_iota(jnp.int32, (1, LANES), 1)
    lane_grp = lax.shift_right_logical(lane, int(math.log2(GROUP_LANES)))
    last_in_grp = (lane & (GROUP_LANES - 1)) == GROUP_LANES - 1
    for pg in range(npc):
        qit_s[pg] = jnp.where(lane_grp == pg, qit_ref[...], 0.0).astype(BF16)
        wq_s[pg] = jnp.where(lane_grp == pg, wq_ref[...], 0.0).astype(BF16)
    w_lane = wl_ref[...]
    kv_rows = kvbuf.shape[2]
    q_per_kv = GROUP_LANES // ATT_KV_HEADS
    kv_of_lane = (lane & (GROUP_LANES - 1)) // q_per_kv
    pairs = lambda n: (lax.broadcasted_iota(jnp.int32, (n, LANES), 0) & (ATT_KV_HEADS - 1)) == kv_of_lane
    first_rows = lambda n: lax.broadcasted_iota(jnp.int32, (n, LANES), 0) < ATT_KV_HEADS

    def idx_score(raw):
        s = jnp.maximum(raw * (IDX_DIM ** -0.5), 0.0) * w_lane
        return _group_sum(s) * (IDX_HEADS ** -0.5)

    def idx_chunk(c, carry):
        job = b * n_chunks + c
        slot = job % ri
        idx_issue(job + ri - 1)
        wait(ci_hbm, ibuf, isem, slot)
        acc = jnp.zeros((PAGE_SIZE, LANES), F32)
        for pg in range(npc):
            acc = acc + jnp.dot(ibuf[slot, pg].astype(BF16), qit_s[pg], preferred_element_type=F32)
        key_ref[c] = _sortable_keys(jnp.where(last_in_grp, idx_score(acc), NEG_INF))
        return carry

    lax.fori_loop(0, n_chunks, idx_chunk, 0)
    rows8 = lambda r: jnp.broadcast_to(r, (8, r.shape[1]))
    new_raw = _bdot(rows8(kin_ref[...]), qit_ref[...])[0:1, :]
    key_new = _sortable_keys(jnp.where(lane == GROUP_LANES - 1, idx_score(new_raw), NEG_INF))

    assert n_chunks <= GROUP_LANES
    dense = key_ref[0]
    for c in range(1, n_chunks):
        dense = jnp.maximum(dense, pltpu.roll(key_ref[c], LANES - c, axis=1))

    def count_ge(t):
        tot = jnp.sum(jnp.where(dense >= t, 1.0, 0.0), axis=0, keepdims=True) + jnp.where(key_new >= t, 1.0, 0.0)
        return jnp.sum(tot, axis=1, keepdims=True)

    thr = _kth_largest_wide(count_ge, topk)
    new_sel = jnp.max(jnp.where(key_new >= thr, 1.0, 0.0), axis=1, keepdims=True) > 0.0

    def k_chunk(c, carry):
        job = b * kv_jobs + c
        slot = job % rk
        kv_issue(job + rk - 1)
        wait(ck_hbm, kvbuf, kvsem, slot)
        acc = jnp.zeros((kv_rows, LANES), F32)
        for pg in range(npc):
            acc = acc + jnp.dot(kvbuf[slot, pg].astype(BF16), wq_s[pg], preferred_element_type=F32)
        sel = _group_spread(jnp.where(key_ref[c] >= thr, 1.0, 0.0))
        sel_rows = jnp.dot(ex_ref[...], sel.astype(BF16), preferred_element_type=F32)
        att_ref[c] = jnp.where(pairs(kv_rows), jnp.where(sel_rows > 0.0, acc * (ATT_HD ** -0.5), NEG_INF), NEG_INF)
        return carry

    lax.fori_loop(0, n_chunks, k_chunk, 0)
    new_pairs = jnp.where(first_rows(8), jnp.where(pairs(8), 1.0, 0.0), 0.0)
    s_new = jnp.sum(new_pairs * _bdot(kn_ref[...], wq_ref[...]), axis=0, keepdims=True)
    s_new = jnp.where(new_sel, s_new * (ATT_HD ** -0.5), NEG_INF)
    m = jnp.max(jnp.max(att_ref[...], axis=0), axis=0, keepdims=True)
    m = jnp.maximum(_across_groups(m, jnp.maximum), s_new)
    e_new = jnp.where(lane < GROUP_LANES, jnp.exp(s_new - m), 0.0)

    def v_chunk(c, carry):
        acc, lsum = carry
        job = b * kv_jobs + n_chunks + c
        slot = job % rk
        kv_issue(job + rk - 1)
        wait(cv_hbm, kvbuf, kvsem, slot)
        et = jnp.exp(att_ref[c] - m).T
        lsum = lsum + jnp.sum(et, axis=1, keepdims=True)
        for pg in range(npc):
            acc = acc + _bdot(et[pg * GROUP_LANES:(pg + 1) * GROUP_LANES, :], kvbuf[slot, pg])
        return acc, lsum

    acc, lsum = lax.fori_loop(0, n_chunks, v_chunk,
                              (jnp.zeros((GROUP_LANES, ATT_HD), F32), jnp.zeros((LANES, 1), F32)))
    top = lambda r: jnp.concatenate([r, jnp.zeros((LANES - r.shape[0], LANES), F32)], axis=0)
    et_new = top(new_pairs * e_new).T
    acc = acc + _bdot(et_new[0:GROUP_LANES, :], top(vn_ref[...]))
    lsum = lsum + jnp.sum(et_new, axis=1, keepdims=True)
    l16 = lsum[0:GROUP_LANES, :]
    for pg in range(1, npc):
        l16 = l16 + lsum[pg * GROUP_LANES:(pg + 1) * GROUP_LANES, :]
    o_ref[...] = (acc / l16).astype(BF16)


def _dsa_step(z2d, cache_k, cache_v, cache_kidx, page_table, *, d_model):
    off = _odd_columns(d_model)
    bsz = z2d.shape[0]
    n_pages = page_table.shape[1]
    n_phys, page, kvh, hd = cache_k.shape
    n_kv = kvh * hd
    heads = d_model // hd
    assert page == PAGE_SIZE and heads == GROUP_LANES and n_pages % PAGES_PER_CHUNK == 0
    n_chunks = n_pages // PAGES_PER_CHUNK
    topk = min(INDEX_TOPK, (n_pages * page + 1) // 4)
    q = z2d[:, off["q"]:off["q"] + d_model].reshape(bsz, kvh, heads // kvh, hd)
    qi = z2d[:, off["qi"]:off["qi"] + IDX_HEADS * IDX_DIM].reshape(bsz, IDX_HEADS, IDX_DIM)
    tile = lambda a: jnp.tile(a, (1, 1, PAGES_PER_CHUNK))
    qit = tile(jnp.transpose(qi, (0, 2, 1)))
    wq = tile(jnp.transpose(q.reshape(bsz, heads, hd), (0, 2, 1)))
    wl = tile(z2d[:, off["wi"]:off["wi"] + IDX_HEADS].reshape(bsz, 1, IDX_HEADS))
    kin = z2d[:, off["ki"]:off["ki"] + IDX_DIM].reshape(bsz, 1, IDX_DIM)
    new_rows = lambda c: jnp.pad(z2d[:, c:c + n_kv].reshape(bsz, kvh, hd), ((0, 0), (0, 8 - kvh), (0, 0)))
    kn, vn = new_rows(off["k"]), new_rows(off["v"])
    expand = jnp.repeat(jnp.eye(page, dtype=BF16), kvh, axis=0)
    per_seq = lambda a: pl.BlockSpec((None,) + a.shape[1:], lambda b, pt: (b, 0, 0))
    hbm = pl.BlockSpec(memory_space=pl.ANY)
    grid_spec = pltpu.PrefetchScalarGridSpec(
        num_scalar_prefetch=1,
        grid=(bsz,),
        in_specs=[per_seq(qit), per_seq(wq), per_seq(wl), per_seq(kin), per_seq(kn), per_seq(vn),
                  pl.BlockSpec(expand.shape, lambda b, pt: (0, 0)), hbm, hbm, hbm],
        out_specs=pl.BlockSpec((None, heads, hd), lambda b, pt: (b, 0, 0)),
        scratch_shapes=[
            pltpu.VMEM((IDX_RING, PAGES_PER_CHUNK, page, IDX_DIM), F32),
            pltpu.VMEM((KV_RING, PAGES_PER_CHUNK, page * kvh, hd), F32),
            pltpu.SemaphoreType.DMA((IDX_RING,)),
            pltpu.SemaphoreType.DMA((KV_RING,)),
            pltpu.VMEM((PAGES_PER_CHUNK, IDX_DIM, LANES), BF16),
            pltpu.VMEM((PAGES_PER_CHUNK, hd, LANES), BF16),
            pltpu.VMEM((n_chunks, page, LANES), jnp.int32),
            pltpu.VMEM((n_chunks, page * kvh, LANES), F32),
        ])
    o = pl.pallas_call(
        functools.partial(_dsa_step_kernel, topk=topk),
        grid_spec=grid_spec,
        out_shape=jax.ShapeDtypeStruct((bsz, heads, hd), BF16),
        compiler_params=_params(("arbitrary",), 40),
        name="dsa_step",
    )(page_table, qit, wq, wl, kin, kn, vn, expand, cache_kidx, cache_k.reshape(n_phys, page * kvh, hd),
      cache_v.reshape(n_phys, page * kvh, hd))
    return o.reshape(bsz, d_model)


def _dsa_prompt(z2d, *, batch, seq, d_model, qb):
    off = _odd_columns(d_model)
    nq = seq // qb
    n_kv = ATT_KV_HEADS * ATT_HD
    topk = min(INDEX_TOPK, seq // 4)
    n_ext = math.gcd(nq, DSA_EXTENTS)
    nqc = nq // n_ext
    outs = []
    for c in range(n_ext):
        q0, nk = c * nqc, (c + 1) * nqc * qb
        rows = lambda w, col, q0=q0: pl.BlockSpec((qb, w), lambda b, i: (b * nq + q0 + i, col // w))
        whole = lambda w, col: pl.BlockSpec((seq, w), lambda b, i: (b, col // w))
        outs.append(pl.pallas_call(
            functools.partial(_dsa_kernel, topk=topk, nk=nk, q0=q0),
            grid=(batch, nqc),
            in_specs=[rows(d_model, off["q"]), rows(IDX_HEADS * IDX_DIM, off["qi"]), rows(LANES, off["wi"]),
                      whole(IDX_DIM, off["ki"]), whole(n_kv, off["k"]), whole(n_kv, off["v"])],
            out_specs=pl.BlockSpec((qb, d_model), lambda b, i: (b * nqc + i, 0)),
            out_shape=jax.ShapeDtypeStruct((batch * nqc * qb, d_model), BF16),
            scratch_shapes=[pltpu.VMEM((nk, IDX_DIM), BF16), pltpu.VMEM((nk, n_kv), BF16), pltpu.VMEM((nk, n_kv), BF16),
                            pltpu.VMEM((qb, nk), jnp.int32), pltpu.VMEM((qb, 1), jnp.int32)],
            compiler_params=_params(("parallel", "arbitrary"), 56),
            name="dsa_prompt",
        )(z2d, z2d, z2d, z2d, z2d, z2d).reshape(batch, nqc * qb, d_model))
    return jnp.concatenate(outs, axis=1).reshape(batch * seq, d_model)


def _trunk(x2d, mods, wts, cfg, even_fn, odd_fn):
    tm, tf, tpb = cfg["tm"], cfg["tf"], cfg["tpb"]
    ffn = functools.partial(_ffn, tm=tm, tf=tf, tiles_per_batch=tpb, vmem_mib=cfg["ffn_vmem"])
    x = x2d
    states = []
    depth = wts["norm_g"].shape[0]
    for layer in range(depth):
        g, mod = wts["norm_g"][layer], mods[layer]
        x = ffn(x, g[0], mod, 0, wts["w_ffn_in"], wts["w_ffn_out"], (layer, 0), wts["final_g"], final_norm=False)
        x, st = (even_fn if layer % 2 == 0 else odd_fn)(x, g[1], mod)
        states.append(st)
        x = ffn(x, g[2], mod, 2, wts["w_ffn_in"], wts["w_ffn_out"], (layer, 1), wts["final_g"],
                final_norm=layer == depth - 1)
    return x, states


def kernel(x_prompt, x_sample, c_prompt, c_sample, state_ret, state_s5_re, state_s5_im, cache_k, cache_v, cache_kidx,
           page_table, norm_g, w_ada, b_ada, w_ffn_in, w_ffn_out, w_in_even, w_out_even, s5_a_re, s5_a_im, s5_log_dt,
           s5_b_re, s5_b_im, s5_c_re, s5_c_im, s5_d, w_glu, b_glu, w_in_odd, w_out_odd, final_g):
    bp, lp, d = x_prompt.shape
    bs, ls, _ = x_sample.shape
    assert ls == 1
    depth = w_ada.shape[0]
    n_mod = N_SUBLAYERS * 3

    rows = -(-(bp + bs) // 8) * 8
    c_all = jnp.concatenate([c_prompt, c_sample, jnp.zeros((rows - bp - bs, d), F32)], axis=0)
    mod_all = _ada_mod(c_all, w_ada, b_ada)
    mods_p = [mod_all[l, :bp].reshape(bp, n_mod, 1, d) for l in range(depth)]
    mods_s = [jnp.transpose(mod_all[l, bp:bp + bs].reshape(bs, n_mod, d), (1, 0, 2))[None] for l in range(depth)]

    p = dict(w_in_even=w_in_even, w_out_even=w_out_even, s5_a_re=s5_a_re, s5_a_im=s5_a_im, s5_log_dt=s5_log_dt,
             s5_b_re=s5_b_re, s5_b_im=s5_b_im, s5_c_re=s5_c_re, s5_c_im=s5_c_im, s5_d=s5_d, w_glu=w_glu, b_glu=b_glu)
    sp = _s5_prepare(p)
    w_odd = _odd_weight(w_in_odd, d)
    off = _odd_columns(d)
    n_kv = ATT_KV_HEADS * ATT_HD
    wts = dict(norm_g=norm_g, w_ffn_in=w_ffn_in, w_ffn_out=w_ffn_out, final_g=final_g)
    odd_tn = 768

    def cache_parts(z, b, l):
        return (z[:, off["k"]:off["k"] + n_kv].reshape(b, l, ATT_KV_HEADS, ATT_HD),
                z[:, off["v"]:off["v"] + n_kv].reshape(b, l, ATT_KV_HEADS, ATT_HD),
                z[:, off["ki"]:off["ki"] + IDX_DIM].reshape(b, l, IDX_DIM))

    cfg_p = dict(tm=min(1024, lp), tf=256, ffn_vmem=60)
    cfg_p["tpb"] = lp // cfg_p["tm"]

    def even_p(x, g, mod):
        zero_s5 = jnp.zeros((bp,) + state_s5_re.shape[1:], F32)
        x, s_ret, sr, si = _even_layer(x, g, mod, p, jnp.zeros((bp,) + state_ret.shape[1:], F32), zero_s5, zero_s5,
                                       batch=bp, seq=lp, pos0=0, tm=cfg_p["tm"], tiles_per_batch=cfg_p["tpb"],
                                       rb=min(256, lp), chunk=min(RET_CHUNK, lp), s5_t=min(128, lp), sp=sp)
        return x, (s_ret, sr, si)

    def odd_p(x, g, mod):
        z = _proj(x, g, mod, 1, w_odd, tm=cfg_p["tm"], tn=odd_tn, tiles_per_batch=cfg_p["tpb"], vmem_mib=52)
        o = _dsa_prompt(z, batch=bp, seq=lp, d_model=d, qb=min(128, lp))
        x = _outproj([(o, d, None)], w_out_odd, x, mod, tm=cfg_p["tm"], tn=1024, tiles_per_batch=cfg_p["tpb"],
                     vmem_mib=48)
        return x, cache_parts(z, bp, lp)

    y_p, (ev_p, od_p) = _trunk(x_prompt.reshape(bp * lp, d), mods_p, wts, cfg_p, even_p, odd_p)

    past_len = page_table.shape[1] * cache_k.shape[1]
    cfg_s = dict(tm=bs, tf=512, tpb=1, ffn_vmem=40)

    def even_s(x, g, mod):
        x, s_ret, sr, si = _even_layer(x, g, mod, p, state_ret, state_s5_re, state_s5_im, batch=bs, seq=1,
                                       pos0=past_len, tm=bs, tiles_per_batch=1, rb=1, chunk=1, s5_t=1, sp=sp)
        return x, (s_ret, sr, si)

    def odd_s(x, g, mod):
        z = _proj(x, g, mod, 1, w_odd, tm=bs, tn=odd_tn, tiles_per_batch=1, vmem_mib=40)
        o = _dsa_step(z, cache_k, cache_v, cache_kidx, page_table, d_model=d)
        x = _outproj([(o, d, None)], w_out_odd, x, mod, tm=bs, tn=1024, tiles_per_batch=1, vmem_mib=40)
        return x, cache_parts(z, bs, 1)

    y_s, (ev_s, od_s) = _trunk(x_sample.reshape(bs, d), mods_s, wts, cfg_s, even_s, odd_s)

    return (y_p.reshape(bp, lp, d), y_s.reshape(bs, 1, d), *ev_p, *od_p, *ev_s, *od_s)
```

```python
import functools
import math

import jax
import jax.numpy as jnp
from jax import lax
from jax.experimental import pallas as pl
from jax.experimental.pallas import tpu as pltpu

F32 = jnp.float32
BF16 = jnp.bfloat16

EPS = 1e-6
FFN_RES = 0.5
N_SUBLAYERS = 3
RET_HEADS = 4
RET_CHUNK = 128
ROPE_BASE = 10000.0
S5_GROUP_CH = 16
S5_STATE = 64
ATT_HD = 128
ATT_KV_HEADS = 4
IDX_HEADS = 16
IDX_DIM = 128
INDEX_TOPK = 256
PAGE_SIZE = 128

LANES = 128
MIB = 1024 * 1024
NEG_INF = float("-inf")


def _params(sem, vmem_mib):
    return pltpu.CompilerParams(dimension_semantics=sem, vmem_limit_bytes=vmem_mib * MIB)


def _bdot(a, b):
    return jnp.dot(a.astype(BF16), b.astype(BF16), preferred_element_type=F32)


def _bdot_nt(a, b):
    return lax.dot_general(a.astype(BF16), b.astype(BF16), (((1,), (1,)), ((), ())),
                           preferred_element_type=F32)


def _bdot_tn(a, b):
    return lax.dot_general(a.astype(BF16), b.astype(BF16), (((0,), (0,)), ((), ())),
                           preferred_element_type=F32)


def _silu(x):
    return x * jax.nn.sigmoid(x)


def _rms(x, g):
    ms = jnp.mean(x * x, axis=-1, keepdims=True)
    return x * lax.rsqrt(ms + EPS) * g


def _row_chunks(n_rows, chunk, fn):
    if n_rows <= chunk:
        fn(slice(0, n_rows))
        return
    assert n_rows % chunk == 0

    def body(i, c):
        fn(pl.ds(pl.multiple_of(i * chunk, chunk), chunk))
        return c

    lax.fori_loop(0, n_rows // chunk, body, 0)


ROW_CHUNK = 256


def _norm_mod_to(h_ref, x_ref, g_ref, sc_ref, sh_ref):
    n_rows = x_ref.shape[0]
    assert sc_ref.shape[0] == 1 or n_rows <= ROW_CHUNK

    def fn(rows):
        y = _rms(x_ref[rows, :], g_ref[...])
        h_ref[rows, :] = (y * (1.0 + sc_ref[...]) + sh_ref[...]).astype(BF16)

    _row_chunks(n_rows, ROW_CHUNK, fn)


def _ada_kernel(c_ref, w_ref, b_ref, o_ref):
    o_ref[...] = _bdot(_silu(c_ref[...]), w_ref[...]) + b_ref[...]


def _ada_mod(c_all, w_ada, b_ada):
    depth, d, n = w_ada.shape
    rows = c_all.shape[0]
    tn = 1024
    return pl.pallas_call(
        _ada_kernel,
        grid=(depth, n // tn),
        in_specs=[
            pl.BlockSpec((rows, d), lambda l, j: (0, 0)),
            pl.BlockSpec((None, d, tn), lambda l, j: (l, 0, j)),
            pl.BlockSpec((None, 1, tn), lambda l, j: (l, 0, j)),
        ],
        out_specs=pl.BlockSpec((None, rows, tn), lambda l, j: (l, 0, j)),
        out_shape=jax.ShapeDtypeStruct((depth, rows, n), F32),
        compiler_params=_params(("arbitrary", "arbitrary"), 40),
        name="ada_mod",
    )(c_all, w_ada, b_ada.reshape(depth, 1, n))


def _mod_specs(mod, sub, tiles_per_batch, which):
    _, _, r, d = mod.shape
    specs = []
    for k in which:
        idx = sub * 3 + k
        specs.append(pl.BlockSpec((None, None, r, d),
                                  lambda i, j, idx=idx: (i // tiles_per_batch, idx, 0, 0)))
    return specs


def _ffn_kernel(x_ref, g_ref, sh_ref, sc_ref, gt_ref, wa_ref, wb_ref, wo_ref, fg_ref, o_ref, h_ref,
                *, final_norm):
    j = pl.program_id(1)
    nj = pl.num_programs(1)

    @pl.when(j == 0)
    def _():
        _norm_mod_to(h_ref, x_ref, g_ref, sc_ref, sh_ref)
        o_ref[...] = jnp.zeros(o_ref.shape, F32)

    h = h_ref[...]
    a = jnp.dot(h, wa_ref[...].astype(BF16), preferred_element_type=F32)
    b = jnp.dot(h, wb_ref[...].astype(BF16), preferred_element_type=F32)
    o_ref[...] += _bdot(_silu(a) * b, wo_ref[...])

    @pl.when(j == nj - 1)
    def _():
        def fn(rows):
            y = x_ref[rows, :] + (FFN_RES * gt_ref[...]) * o_ref[rows, :]
            if final_norm:
                y = _rms(y, fg_ref[...])
            o_ref[rows, :] = y

        _row_chunks(x_ref.shape[0], ROW_CHUNK, fn)


def _ffn(x2d, norm_g, mod, sub, w_in, w_out, wsel, final_g, *, tm, tf, tiles_per_batch, final_norm, vmem_mib):
    m, d = x2d.shape
    f = w_out.shape[2]
    nf = f // tf
    assert m % tm == 0 and f % tf == 0
    row = lambda i, j: (0, 0)
    l, k = wsel
    return pl.pallas_call(
        functools.partial(_ffn_kernel, final_norm=final_norm),
        grid=(m // tm, nf),
        in_specs=[
            pl.BlockSpec((tm, d), lambda i, j: (i, 0), pipeline_mode=pl.Buffered(1)),
            pl.BlockSpec((1, d), row),
            *_mod_specs(mod, sub, tiles_per_batch, (0, 1, 2)),
            pl.BlockSpec((None, None, d, tf), lambda i, j: (l, k, 0, j)),
            pl.BlockSpec((None, None, d, tf), lambda i, j: (l, k, 0, nf + j)),
            pl.BlockSpec((None, None, tf, d), lambda i, j: (l, k, j, 0)),
            pl.BlockSpec((1, d), row),
        ],
        out_specs=pl.BlockSpec((tm, d), lambda i, j: (i, 0)),
        out_shape=jax.ShapeDtypeStruct((m, d), F32),
        scratch_shapes=[pltpu.VMEM((tm, d), BF16)],
        compiler_params=_params(("parallel", "arbitrary"), vmem_mib),
        name="ffn",
    )(x2d, norm_g.reshape(1, d), mod, mod, mod, w_in, w_in, w_out, final_g.reshape(1, d))


def _proj_kernel(x_ref, g_ref, sh_ref, sc_ref, w_ref, o_ref, h_ref):
    @pl.when(pl.program_id(1) == 0)
    def _():
        _norm_mod_to(h_ref, x_ref, g_ref, sc_ref, sh_ref)

    o_ref[...] = jnp.dot(h_ref[...], w_ref[...].astype(BF16), preferred_element_type=F32)


def _proj(x2d, norm_g, mod, sub, w, *, tm, tn, tiles_per_batch, vmem_mib):
    m, d = x2d.shape
    n = w.shape[1]
    assert m % tm == 0 and n % tn == 0
    return pl.pallas_call(
        _proj_kernel,
        grid=(m // tm, n // tn),
        in_specs=[
            pl.BlockSpec((tm, d), lambda i, j: (i, 0)),
            pl.BlockSpec((1, d), lambda i, j: (0, 0)),
            *_mod_specs(mod, sub, tiles_per_batch, (0, 1)),
            pl.BlockSpec((d, tn), lambda i, j: (0, j)),
        ],
        out_specs=pl.BlockSpec((tm, tn), lambda i, j: (i, j)),
        out_shape=jax.ShapeDtypeStruct((m, n), F32),
        scratch_shapes=[pltpu.VMEM((tm, d), BF16)],
        compiler_params=_params(("parallel", "arbitrary"), vmem_mib),
        name="mixer_in_proj",
    )(x2d, norm_g.reshape(1, d), mod, mod, w)


def _outproj_kernel(*refs, n_parts):
    a_refs = refs[:n_parts]
    w_ref, x_ref, gt_ref, o_ref = refs[n_parts:]
    acc = None
    k0 = 0
    for a_ref in a_refs:
        kk = a_ref.shape[1]
        p = jnp.dot(a_ref[...], w_ref[k0:k0 + kk, :].astype(BF16), preferred_element_type=F32)
        acc = p if acc is None else acc + p
        k0 += kk
    o_ref[...] = x_ref[...] + gt_ref[...] * acc


def _outproj(a_parts, w, x2d, mod, *, tm, tn, tiles_per_batch, vmem_mib):
    m, d = x2d.shape
    k = w.shape[0]
    assert sum(kw for _, kw, _ in a_parts) == k and m % tm == 0 and d % tn == 0
    _, _, r, _ = mod.shape
    return pl.pallas_call(
        functools.partial(_outproj_kernel, n_parts=len(a_parts)),
        grid=(m // tm, d // tn),
        in_specs=[
            *[pl.BlockSpec((tm, kw), imap or (lambda i, j: (i, 0))) for _, kw, imap in a_parts],
            pl.BlockSpec((k, tn), lambda i, j: (0, j)),
            pl.BlockSpec((tm, tn), lambda i, j: (i, j)),
            pl.BlockSpec((None, None, r, tn), lambda i, j: (i // tiles_per_batch, 5, 0, j)),
        ],
        out_specs=pl.BlockSpec((tm, tn), lambda i, j: (i, j)),
        out_shape=jax.ShapeDtypeStruct((m, d), F32),
        compiler_params=_params(("parallel", "arbitrary"), vmem_mib),
        name="mixer_out_proj",
    )(*[a for a, _, _ in a_parts], w, x2d, mod)


def _rot(x1, x2, cos, sin):
    return jnp.concatenate([x1 * cos - x2 * sin, x1 * sin + x2 * cos], axis=-1)


def _head_norm(o):
    mu = jnp.mean(o, axis=-1, keepdims=True)
    var = jnp.mean(jnp.square(o - mu), axis=-1, keepdims=True)
    return (o - mu) * lax.rsqrt(var + 1e-5)


def _ret_log_g(h):
    return math.log1p(-(2.0 ** (-5.0 - h)))


def _ret_kernel(q_ref, k_ref, v_ref, g_ref, cos_ref, sin_ref, s0_ref, o_ref, s_ref, *, chunk):
    c = pl.program_id(1)
    rb = q_ref.shape[0]
    dk = q_ref.shape[1] // RET_HEADS
    half = dk // 2

    @pl.when(c == 0)
    def _():
        s_ref[...] = s0_ref[...]

    ri = lax.broadcasted_iota(jnp.int32, (chunk, chunk), 0).astype(F32)
    ci = lax.broadcasted_iota(jnp.int32, (chunk, chunk), 1).astype(F32)
    diff = ri - ci
    rowf = lax.broadcasted_iota(jnp.int32, (chunk, dk), 0).astype(F32)
    for h in range(RET_HEADS):
        lg = _ret_log_g(h)
        dmat = jnp.where(diff >= 0, jnp.exp(jnp.maximum(diff, 0.0) * lg), 0.0)
        q_dec = jnp.exp((rowf + 1.0) * lg)
        k_dec = jnp.exp((chunk - 1.0 - rowf) * lg)
        c_dec = math.exp(chunk * lg)
        c0 = h * dk
        for ck in range(rb // chunk):
            rows = slice(ck * chunk, (ck + 1) * chunk)
            cos = cos_ref[rows, :]
            sin = sin_ref[rows, :]
            qr = _rot(q_ref[rows, c0:c0 + half], q_ref[rows, c0 + half:c0 + dk], cos, sin)
            kr = _rot(k_ref[rows, c0:c0 + half], k_ref[rows, c0 + half:c0 + dk], cos, sin) * (dk ** -0.5)
            vb = v_ref[rows, c0:c0 + dk].astype(BF16)
            qb = qr.astype(BF16)
            a = _bdot_nt(qb, kr) * dmat
            s = s_ref[h]
            o = _bdot(a, vb) + _bdot(qb, s) * q_dec
            s_ref[h] = s * c_dec + _bdot_tn(kr * k_dec, vb)
            o_ref[rows, c0:c0 + dk] = (_head_norm(o) * _silu(g_ref[rows, c0:c0 + dk])).astype(BF16)


def _retention(z2d, col0, cos, sin, state0, *, batch, seq, rb, chunk):
    _, h, dk, dv = state0.shape
    w = h * dk
    nb = seq // rb
    assert seq % rb == 0 and rb % chunk == 0 and col0 % w == 0
    cb = col0 // w
    zspec = lambda k: pl.BlockSpec((rb, w), lambda b, c, k=k: (b * nb + c, cb + k))
    tab = pl.BlockSpec((rb, dk // 2), lambda b, c: (c, 0))
    st = pl.BlockSpec((None, h, dk, dv), lambda b, c: (b, 0, 0, 0))
    return pl.pallas_call(
        functools.partial(_ret_kernel, chunk=chunk),
        grid=(batch, nb),
        in_specs=[zspec(0), zspec(1), zspec(2), zspec(3), tab, tab, st],
        out_specs=[pl.BlockSpec((rb, w), lambda b, c: (b * nb + c, 0)), st],
        out_shape=[jax.ShapeDtypeStruct((batch * seq, w), BF16),
                   jax.ShapeDtypeStruct(state0.shape, F32)],
        compiler_params=_params(("parallel", "arbitrary"), 40),
        name="retention",
    )(z2d, z2d, z2d, z2d, cos, sin, state0)


def _rope_tables(pos, half):
    freqs = ROPE_BASE ** (-jnp.arange(half, dtype=F32) / half)
    ang = pos.astype(F32)[:, None] * freqs[None, :]
    return jnp.cos(ang), jnp.sin(ang)


def _r16(x):
    return x.astype(BF16).astype(F32)


def _ret_step_kernel(q_ref, krow_ref, kcol_ref, v_ref, g_ref, cos_ref, sin_ref, cosc_ref, sinc_ref, s0_ref,
                     o_ref, s_ref):
    dk = q_ref.shape[1] // RET_HEADS
    half = dk // 2
    cos, sin = cos_ref[...], sin_ref[...]
    cosc, sinc = cosc_ref[...], sinc_ref[...]
    for h in range(RET_HEADS):
        dec = math.exp(_ret_log_g(h))
        c0 = h * dk
        qr = _r16(_rot(q_ref[:, c0:c0 + half], q_ref[:, c0 + half:c0 + dk], cos, sin))
        kr = _r16(_rot(krow_ref[:, c0:c0 + half], krow_ref[:, c0 + half:c0 + dk], cos, sin) * (dk ** -0.5))
        k1, k2 = kcol_ref[h, 0:half, :], kcol_ref[h, half:dk, :]
        kc = _r16(jnp.concatenate([k1 * cosc - k2 * sinc, k1 * sinc + k2 * cosc], axis=0) * (dk ** -0.5))
        v = _r16(v_ref[:, c0:c0 + dk])
        a = jnp.sum(qr * kr, axis=-1, keepdims=True)
        s = s0_ref[h]
        o = _r16(a) * v + _bdot(qr, s) * dec
        s_ref[h] = s * dec + kc * v
        o_ref[:, c0:c0 + dk] = (_head_norm(o) * _silu(g_ref[:, c0:c0 + dk])).astype(BF16)


def _retention_step(z2d, cos, sin, state0):
    b, h, dk, dv = state0.shape
    w = h * dk
    z3 = z2d.reshape(b, 1, z2d.shape[1])
    kcol = z2d[:, w:2 * w].reshape(b, h, dk, 1)
    half = dk // 2
    zspec = lambda k: pl.BlockSpec((None, 1, w), lambda i, k=k: (i, 0, k))
    row = pl.BlockSpec((1, half), lambda i: (0, 0))
    col = pl.BlockSpec((half, 1), lambda i: (0, 0))
    st = pl.BlockSpec((None, h, dk, dv), lambda i: (i, 0, 0, 0))
    og, s = pl.pallas_call(
        _ret_step_kernel,
        grid=(b,),
        in_specs=[zspec(0), zspec(1), pl.BlockSpec((None, h, dk, 1), lambda i: (i, 0, 0, 0)), zspec(2), zspec(3),
                  row, row, col, col, st],
        out_specs=[pl.BlockSpec((None, 1, w), lambda i: (i, 0, 0)), st],
        out_shape=[jax.ShapeDtypeStruct((b, 1, w), BF16), jax.ShapeDtypeStruct(state0.shape, F32)],
        compiler_params=_params(("parallel",), 32),
        name="retention_step",
    )(z3, z3, kcol, z3, z3, cos, sin, cos.reshape(half, 1), sin.reshape(half, 1), state0)
    return og.reshape(b, w), s


S5_BLOCKS = 8


def _s5_disc(ar, ai, dt):
    mag = jnp.exp(dt * ar)
    abr = mag * jnp.cos(dt * ai)
    abi = mag * jnp.sin(dt * ai)
    den = ar * ar + ai * ai
    xr = abr - 1.0
    return abr, abi, (xr * ar + abi * ai) / den, (abi * ar - xr * ai) / den


def _s5_param_kernel(ar_ref, ai_ref, ldt_ref, arr_ref, air_ref, br_ref, bi_ref, abr_ref, abi_ref, bbr_ref, bbi_ref):
    dt = jnp.exp(ldt_ref[...])
    abr_ref[...], abi_ref[...], _, _ = _s5_disc(ar_ref[...], ai_ref[...], dt)
    _, _, fr, fi = _s5_disc(arr_ref[...], air_ref[...], dt)
    br, bi = br_ref[...], bi_ref[...]
    bbr_ref[...] = fr * br - fi * bi
    bbi_ref[...] = fr * bi + fi * br


def _s5_prepare(p):
    g, st = p["s5_a_re"].shape
    ch = p["s5_d"].shape[1]
    rep = lambda a: jnp.repeat(a, ch, axis=1)
    abr, abi, bbr, bbi = pl.pallas_call(
        _s5_param_kernel,
        out_shape=[jax.ShapeDtypeStruct((g, st), F32)] * 2 + [jax.ShapeDtypeStruct((g, st * ch), F32)] * 2,
        name="s5_discretise",
    )(p["s5_a_re"], p["s5_a_im"], p["s5_log_dt"].reshape(g, 1), rep(p["s5_a_re"]), rep(p["s5_a_im"]),
      p["s5_b_re"].reshape(g, st * ch), p["s5_b_im"].reshape(g, st * ch))
    nb = S5_BLOCKS
    gl = g // nb
    eye = jnp.eye(gl, dtype=F32)
    w_in = lambda bb: jnp.einsum("bgpc,gh->bgchp", bb.reshape(nb, gl, st, ch), eye).reshape(nb, gl * ch, gl * st)
    w_out = lambda c: jnp.einsum("bgcp,gh->bgphc", c.reshape(nb, gl, ch, st), eye).reshape(nb, gl * st, gl * ch)
    return dict(
        abr=abr.reshape(1, g * st), abi=abi.reshape(1, g * st),
        wbr=w_in(bbr).astype(BF16), wbi=w_in(bbi).astype(BF16),
        wcr=w_out(p["s5_c_re"]).astype(BF16), wci=w_out(p["s5_c_im"]).astype(BF16),
        d=p["s5_d"].reshape(1, g * ch), w_glu=p["w_glu"], b_glu=p["b_glu"].reshape(1, -1))


S5_SCAN_LANES = 1024


def _s5_kernel(u_ref, x0r_ref, x0i_ref, abr_ref, abi_ref, wbr_ref, wbi_ref, wcr_ref, wci_ref, d_ref, wg_ref,
               bg_ref, y_ref, xr_ref, xi_ref, bur, bui, yb, wgb, *, ns, t):
    nblk = wbr_ref.shape[0]
    uw, sw = wbr_ref.shape[1], wbr_ref.shape[2]
    n_tiles = nblk * sw // LANES
    sub = 8

    @pl.when(pl.program_id(0) == 0)
    def _():
        xr_ref[...] = x0r_ref[...]
        xi_ref[...] = x0i_ref[...]
        wgb[...] = wg_ref[...].astype(BF16)

    tiles_per_blk = sw // LANES
    lane_tile = lambda j: slice(j * LANES, (j + 1) * LANES)
    for b in range(nblk):
        ub = u_ref[:, b * uw:(b + 1) * uw].astype(BF16)
        pr = jnp.dot(ub, wbr_ref[b], preferred_element_type=F32)
        pi = jnp.dot(ub, wbi_ref[b], preferred_element_type=F32)
        for k in range(tiles_per_blk):
            bur[b * tiles_per_blk + k] = pr[:, lane_tile(k)]
            bui[b * tiles_per_blk + k] = pi[:, lane_tile(k)]

    def affine(ar, ai, xr, xi, vr, vi):
        return ar * xr - ai * xi + vr, ar * xi + ai * xr + vi

    per_pass = S5_SCAN_LANES // LANES
    for j0 in range(0, n_tiles, per_pass):
        tiles = list(range(j0, min(j0 + per_pass, n_tiles)))
        if t == 1:
            for j in tiles:
                ar = jnp.broadcast_to(abr_ref[:, lane_tile(j)], (ns, LANES))
                ai = jnp.broadcast_to(abi_ref[:, lane_tile(j)], (ns, LANES))
                nr, ni = affine(ar, ai, xr_ref[:, lane_tile(j)], xi_ref[:, lane_tile(j)], bur[j], bui[j])
                bur[j], bui[j] = nr, ni
                xr_ref[:, lane_tile(j)], xi_ref[:, lane_tile(j)] = nr, ni
            continue
        assert 2 * ns == sub and t % 2 == 0
        ar = [jnp.broadcast_to(abr_ref[:, lane_tile(j)], (sub, LANES)) for j in tiles]
        ai = [jnp.broadcast_to(abi_ref[:, lane_tile(j)], (sub, LANES)) for j in tiles]
        top = lax.broadcasted_iota(jnp.int32, (sub, LANES), 0) < ns
        swap = lambda x: pltpu.roll(x, ns, axis=0)

        def step(i, carry, tiles=tiles, ar=ar, ai=ai):
            rows = pl.ds(pl.multiple_of(i * sub, sub), sub)
            out = []
            for k, j in enumerate(tiles):
                pr_, pi_ = carry[k]
                vr, vi = bur[j, rows, :], bui[j, rows, :]
                er, ei = affine(ar[k], ai[k], swap(pr_), swap(pi_), vr, vi)
                orr, oi = affine(ar[k], ai[k], swap(er), swap(ei), vr, vi)
                nr, ni = jnp.where(top, er, orr), jnp.where(top, ei, oi)
                bur[j, rows, :] = nr
                bui[j, rows, :] = ni
                out.append((nr, ni))
            return tuple(out)

        twice = lambda x: jnp.concatenate([x, x], axis=0)
        fin = lax.fori_loop(0, t // 2, step,
                            tuple((twice(xr_ref[:, lane_tile(j)]), twice(xi_ref[:, lane_tile(j)])) for j in tiles))
        for k, j in enumerate(tiles):
            xr_ref[:, lane_tile(j)] = fin[k][0][ns:, :]
            xi_ref[:, lane_tile(j)] = fin[k][1][ns:, :]

    blk_states = lambda ref, b: jnp.concatenate(
        [ref[b * tiles_per_blk + k] for k in range(tiles_per_blk)], axis=1).astype(BF16)
    for b in range(nblk):
        yb[:, b * uw:(b + 1) * uw] = (jnp.dot(blk_states(bur, b), wcr_ref[b], preferred_element_type=F32)
                                      - jnp.dot(blk_states(bui, b), wci_ref[b], preferred_element_type=F32))
    gl = jax.nn.gelu(yb[...] + d_ref[...] * u_ref[...])
    gate = jax.nn.sigmoid(jnp.dot(gl.astype(BF16), wgb[...], preferred_element_type=F32) + bg_ref[...])
    y_ref[...] = (gl * gate).astype(BF16)


def _s5(u2d, col_blk, sp, x0r, x0i, *, batch, t):
    w = sp["d"].shape[1]
    n_state = sp["abr"].shape[1]
    rows = batch * t
    assert u2d.shape[0] % rows == 0
    full = lambda a: pl.BlockSpec(a.shape, lambda c, nd=a.ndim: (0,) * nd)
    st_spec = pl.BlockSpec((batch, n_state), lambda c: (0, 0))
    consts = [sp["abr"], sp["abi"], sp["wbr"], sp["wbi"], sp["wcr"], sp["wci"], sp["d"], sp["w_glu"], sp["b_glu"]]
    return pl.pallas_call(
        functools.partial(_s5_kernel, ns=batch, t=t),
        grid=(u2d.shape[0] // rows,),
        in_specs=[pl.BlockSpec((rows, w), lambda c: (c, col_blk)), st_spec, st_spec] + [full(a) for a in consts],
        out_specs=[pl.BlockSpec((rows, w), lambda c: (c, 0)), st_spec, st_spec],
        out_shape=[jax.ShapeDtypeStruct((u2d.shape[0], w), BF16), jax.ShapeDtypeStruct((batch, n_state), F32),
                   jax.ShapeDtypeStruct((batch, n_state), F32)],
        scratch_shapes=[pltpu.VMEM((n_state // LANES, rows, LANES), F32), pltpu.VMEM((n_state // LANES, rows, LANES), F32),
                        pltpu.VMEM((rows, w), F32), pltpu.VMEM(sp["w_glu"].shape, BF16)],
        compiler_params=_params(("arbitrary",), 48),
        name="s5",
    )(u2d, x0r.reshape(batch, n_state), x0i.reshape(batch, n_state), *consts)


def _even_layer(x2d, norm_g, mod, p, ret0, s5r0, s5i0, *, batch, seq, pos0, tm, tiles_per_batch, rb, chunk, s5_t,
                sp=None):
    _, h, dk, _ = ret0.shape
    rw = h * dk
    z = _proj(x2d, norm_g, mod, 1, p["w_in_even"], tm=tm, tn=1024, tiles_per_batch=tiles_per_batch, vmem_mib=52)
    cos, sin = _rope_tables(pos0 + jnp.arange(seq), dk // 2)
    if seq > 1:
        og, s_ret = _retention(z, 0, cos, sin, ret0, batch=batch, seq=seq, rb=rb, chunk=chunk)
    else:
        og, s_ret = _retention_step(z, cos, sin, ret0)
    sp = _s5_prepare(p) if sp is None else sp
    sw = sp["d"].shape[1]
    if seq > 1:
        u2d = jnp.transpose(z.reshape(batch, seq, -1)[:, :, 4 * rw:4 * rw + sw], (1, 0, 2)).reshape(seq * batch, sw)
        y, sr, si = _s5(u2d, 0, sp, s5r0, s5i0, batch=batch, t=s5_t)
        y_part = (y.reshape(seq, batch * sw), sw, lambda i, j: (i % tiles_per_batch, i // tiles_per_batch))
    else:
        y, sr, si = _s5(z, (4 * rw) // sw, sp, s5r0, s5i0, batch=batch, t=1)
        y_part = (y, sw, None)
    x_new = _outproj([(og, rw, None), y_part], p["w_out_even"], x2d, mod, tm=tm, tn=1024,
                     tiles_per_batch=tiles_per_batch, vmem_mib=48)
    return x_new, s_ret, sr.reshape(s5r0.shape), si.reshape(s5i0.shape)


INT32_MIN = -2 ** 31


def _sortable_keys(score):
    score = jnp.where(score == 0.0, 0.0, score)
    bits = lax.bitcast_convert_type(score, jnp.int32)
    return bits ^ (lax.shift_right_arithmetic(bits, 31) & 0x7FFFFFFF)


def _kth_largest(count_ge, shape, k):
    kf = jnp.float32(k)
    prefix = jnp.where(count_ge(jnp.zeros(shape, jnp.int32)) >= kf, 0, INT32_MIN).astype(jnp.int32)

    def body(i, prefix):
        cand = prefix | lax.shift_left(jnp.int32(1), 30 - i)
        return jnp.where(count_ge(cand) >= kf, cand, prefix)

    return lax.fori_loop(0, 31, body, prefix)


def _kth_largest_wide(count_ge, k, digit_bits=4):
    kf = jnp.float32(k)
    prefix = jnp.where(count_ge(jnp.zeros((1, 1), jnp.int32)) >= kf, 0, INT32_MIN).astype(jnp.int32)
    hi = 31
    while hi > 0:
        lo = max(hi - digit_bits, 0)
        digit = jnp.zeros((1, 1), jnp.int32)
        for d in range(1, 1 << (hi - lo)):
            digit = digit + jnp.where(count_ge(prefix | (d << lo)) >= kf, 1, 0)
        prefix = prefix | (digit * (1 << lo))
        hi = lo
    return prefix


def _odd_columns(d_model):
    n_q = d_model
    n_kv = ATT_KV_HEADS * ATT_HD
    n_qi = IDX_HEADS * IDX_DIM
    off = dict(q=0, qi=n_q, k=n_q + n_qi, v=n_q + n_qi + n_kv, ki=n_q + n_qi + 2 * n_kv)
    off["wi"] = off["ki"] + IDX_DIM
    off["end"] = off["wi"] + LANES
    return off


def _odd_weight(w_in_odd, d_model):
    n_kv = ATT_KV_HEADS * ATT_HD
    n_qi = IDX_HEADS * IDX_DIM
    cuts = [d_model, d_model + n_kv, d_model + 2 * n_kv, d_model + 2 * n_kv + n_qi, d_model + 2 * n_kv + n_qi + IDX_DIM]
    q, k, v, qi, ki, wi = jnp.split(w_in_odd, cuts, axis=1)
    pad = jnp.zeros((w_in_odd.shape[0], LANES - wi.shape[1]), w_in_odd.dtype)
    return jnp.concatenate([q, qi, k, v, ki, wi, pad], axis=1)


DSA_EXTENTS = 4


def _dsa_kernel(q_ref, qi_ref, wi_ref, ki_ref, k_ref, v_ref, o_ref, kib, kb, vb, key_ref, cut_ref, *, topk, nk, q0):
    i = pl.program_id(1)
    qb = q_ref.shape[0]
    group = q_ref.shape[1] // ATT_HD // ATT_KV_HEADS

    @pl.when(i == 0)
    def _():
        kib[...] = ki_ref[0:nk, :].astype(BF16)
        kb[...] = k_ref[0:nk, :].astype(BF16)
        vb[...] = v_ref[0:nk, :].astype(BF16)

    score = jnp.zeros((qb, nk), F32)
    for h in range(IDX_HEADS):
        s = _bdot_nt(qi_ref[:, h * IDX_DIM:(h + 1) * IDX_DIM], kib[...]) * (IDX_DIM ** -0.5)
        score = score + jnp.maximum(s, 0.0) * wi_ref[:, h:h + 1]
    score = score * (IDX_HEADS ** -0.5)
    q_pos = (q0 + i) * qb + lax.broadcasted_iota(jnp.int32, (qb, nk), 0)
    col = lax.broadcasted_iota(jnp.int32, (qb, nk), 1)
    key_ref[...] = _sortable_keys(jnp.where(col <= q_pos, score, NEG_INF))

    count = lambda hit: jnp.sum(hit, axis=-1, keepdims=True)
    one = lambda cond: jnp.where(cond, 1.0, 0.0)
    thr = _kth_largest(lambda t: count(one(key_ref[...] >= t)), (qb, 1), topk)

    cut_ref[...] = jnp.full((qb, 1), nk, jnp.int32)

    @pl.when(jnp.max(count(one(key_ref[...] >= thr))) > topk)
    def _():
        need = topk - count(one(key_ref[...] > thr))

        def bit(j, cut):
            cand = cut + lax.shift_left(jnp.int32(1), nk.bit_length() - 1 - j)
            below = count(jnp.where(key_ref[...] == thr, one(col < cand), 0.0))
            return jnp.where(below < need, cand, cut)

        cut_ref[...] = lax.fori_loop(0, nk.bit_length(), bit, jnp.zeros((qb, 1), jnp.int32))

    kt = key_ref[...]
    keep = jnp.where(kt > thr, 0.0, jnp.where(kt == thr, jnp.where(col <= cut_ref[...], 0.0, NEG_INF), NEG_INF))
    bias = jnp.concatenate([jnp.where(col <= q_pos, keep, NEG_INF)] * group, axis=0)
    for n in range(ATT_KV_HEADS):
        heads = range(n * group, (n + 1) * group)
        qs = jnp.concatenate([q_ref[:, h * ATT_HD:(h + 1) * ATT_HD] for h in heads], axis=0)
        s = _bdot_nt(qs, kb[:, n * ATT_HD:(n + 1) * ATT_HD]) * (ATT_HD ** -0.5) + bias
        e = jnp.exp(s - jnp.max(s, axis=-1, keepdims=True))
        o = _bdot(e, vb[:, n * ATT_HD:(n + 1) * ATT_HD]) / jnp.sum(e, axis=-1, keepdims=True)
        for g, h in enumerate(heads):
            o_ref[:, h * ATT_HD:(h + 1) * ATT_HD] = o[g * qb:(g + 1) * qb, :].astype(BF16)


PAGES_PER_CHUNK = 8
GROUP_LANES = LANES // PAGES_PER_CHUNK
IDX_RING = 4
KV_RING = 8
DECODE_UNROLL = 2


def _group_sum(x):
    sh = 1
    while sh < GROUP_LANES:
        x = x + pltpu.roll(x, sh, axis=1)
        sh *= 2
    return x


def _group_spread(x):
    sh = 1
    while sh < GROUP_LANES:
        x = jnp.maximum(x, pltpu.roll(x, LANES - sh, axis=1))
        sh *= 2
    return x


def _across_groups(x, op):
    sh = GROUP_LANES
    while sh < LANES:
        x = op(x, pltpu.roll(x, sh, axis=1))
        sh *= 2
    return x


def _dsa_step_kernel(pt_ref, qit_ref, wq_ref, wl_ref, kin_ref, kn_ref, vn_ref, ex_ref, ci_hbm, ck_hbm, cv_hbm, o_ref,
                     ibuf, kvbuf, isem, kvsem, qit_s, wq_s, key_ref, att_ref, *, topk):
    b = pl.program_id(0)
    nb = pl.num_programs(0)
    npc = PAGES_PER_CHUNK
    n_chunks = key_ref.shape[0]
    assert GROUP_LANES == IDX_HEADS and n_chunks % DECODE_UNROLL == 0 and min(ibuf.shape[0], kvbuf.shape[0]) > DECODE_UNROLL

    def page_copy(src, buf, sem, seq, chunk, slot, pg):
        return pltpu.make_async_copy(src.at[pt_ref[seq, chunk * npc + pg]], buf.at[slot, pg], sem.at[slot])

    def start(src, buf, sem, seq, chunk, slot):
        for pg in range(npc):
            page_copy(src, buf, sem, seq, chunk, slot, pg).start()

    def wait(src, buf, sem, slot):
        for pg in range(npc):
            page_copy(src, buf, sem, 0, 0, slot, pg).wait()

    ri, rk = ibuf.shape[0], kvbuf.shape[0]
    kv_jobs = 2 * n_chunks

    def idx_issue(job):
        @pl.when(job < nb * n_chunks)
        def _():
            start(ci_hbm, ibuf, isem, job // n_chunks, job % n_chunks, job % ri)

    def kv_issue(job):
        seq, jj, slot = job // kv_jobs, job % kv_jobs, job % rk

        @pl.when(jnp.logical_and(job < nb * kv_jobs, jj < n_chunks))
        def _():
            start(ck_hbm, kvbuf, kvsem, seq, jj, slot)

        @pl.when(jnp.logical_and(job < nb * kv_jobs, jj >= n_chunks))
        def _():
            start(cv_hbm, kvbuf, kvsem, seq, jj - n_chunks, slot)

    @pl.when(b == 0)
    def _():
        for job in range(ri - DECODE_UNROLL):
            idx_issue(jnp.int32(job))
        for job in range(rk - DECODE_UNROLL):
            kv_issue(jnp.int32(job))

    def stream_group(g, first_job, ring, issue, src, buf, sem):
        jobs = [first_job + g * DECODE_UNROLL + k for k in range(DECODE_UNROLL)]
        for job in jobs:
            issue(job + ring - DECODE_UNROLL)
        for job in jobs:
            wait(src, buf, sem, job % ring)
        return [(g * DECODE_UNROLL + k, job % ring) for k, job in enumerate(jobs)]

    lane = lax.broadcasted_iota(jnp.int32, (1, LANES), 1)
    lane_grp = lax.shift_right_logical(lane, int(math.log2(GROUP_LANES)))
    last_in_grp = (lane & (GROUP_LANES - 1)) == GROUP_LANES - 1
    for pg in range(npc):
        qit_s[pg * IDX_DIM:(pg + 1) * IDX_DIM, :] = jnp.where(lane_grp == pg, qit_ref[...], 0.0).astype(BF16)
        wq_s[pg * ATT_HD:(pg + 1) * ATT_HD, :] = jnp.where(lane_grp == pg, wq_ref[...], 0.0).astype(BF16)
    w_lane = wl_ref[...]
    kv_rows = kvbuf.shape[2]
    q_per_kv = GROUP_LANES // ATT_KV_HEADS
    kv_of_lane = (lane & (GROUP_LANES - 1)) // q_per_kv
    pairs = lambda n: (lax.broadcasted_iota(jnp.int32, (n, LANES), 0) & (ATT_KV_HEADS - 1)) == kv_of_lane
    first_rows = lambda n: lax.broadcasted_iota(jnp.int32, (n, LANES), 0) < ATT_KV_HEADS

    def idx_score(raw):
        s = jnp.maximum(raw * (IDX_DIM ** -0.5), 0.0) * w_lane
        return _group_sum(s) * (IDX_HEADS ** -0.5)

    def idx_chunks(g, carry):
        for c, slot in stream_group(g, b * n_chunks, ri, idx_issue, ci_hbm, ibuf, isem):
            pages = jnp.concatenate([ibuf[slot, pg].astype(BF16) for pg in range(npc)], axis=1)
            acc = jnp.dot(pages, qit_s[...], preferred_element_type=F32)
            key_ref[c] = _sortable_keys(jnp.where(last_in_grp, idx_score(acc), NEG_INF))
        return carry

    lax.fori_loop(0, n_chunks // DECODE_UNROLL, idx_chunks, 0)
    rows8 = lambda r: jnp.broadcast_to(r, (8, r.shape[1]))
    new_raw = _bdot(rows8(kin_ref[...]), qit_ref[...])[0:1, :]
    key_new = _sortable_keys(jnp.where(lane == GROUP_LANES - 1, idx_score(new_raw), NEG_INF))

    assert n_chunks <= GROUP_LANES
    dense = key_ref[0]
    for c in range(1, n_chunks):
        dense = jnp.maximum(dense, pltpu.roll(key_ref[c], LANES - c, axis=1))

    def count_ge(t):
        tot = jnp.sum(jnp.where(dense >= t, 1.0, 0.0), axis=0, keepdims=True) + jnp.where(key_new >= t, 1.0, 0.0)
        return jnp.sum(tot, axis=1, keepdims=True)

    thr = _kth_largest_wide(count_ge, topk)
    new_sel = jnp.max(jnp.where(key_new >= thr, 1.0, 0.0), axis=1, keepdims=True) > 0.0

    def k_chunks(g, carry):
        for c, slot in stream_group(g, b * kv_jobs, rk, kv_issue, ck_hbm, kvbuf, kvsem):
            pages = jnp.concatenate([kvbuf[slot, pg].astype(BF16) for pg in range(npc)], axis=1)
            acc = jnp.dot(pages, wq_s[...], preferred_element_type=F32)
            sel = _group_spread(jnp.where(key_ref[c] >= thr, 1.0, 0.0))
            sel_rows = jnp.dot(ex_ref[...], sel.astype(BF16), preferred_element_type=F32)
            att_ref[c] = jnp.where(pairs(kv_rows), jnp.where(sel_rows > 0.0, acc * (ATT_HD ** -0.5), NEG_INF),
                                   NEG_INF)
        return carry

    lax.fori_loop(0, n_chunks // DECODE_UNROLL, k_chunks, 0)
    new_pairs = jnp.where(first_rows(8), jnp.where(pairs(8), 1.0, 0.0), 0.0)
    s_new = jnp.sum(new_pairs * _bdot(kn_ref[...], wq_ref[...]), axis=0, keepdims=True)
    s_new = jnp.where(new_sel, s_new * (ATT_HD ** -0.5), NEG_INF)
    m = jnp.max(jnp.max(att_ref[...], axis=0), axis=0, keepdims=True)
    m = jnp.maximum(_across_groups(m, jnp.maximum), s_new)
    e_new = jnp.where(lane < GROUP_LANES, jnp.exp(s_new - m), 0.0)

    def v_chunks(g, carry):
        acc, lsum = carry
        for c, slot in stream_group(g, b * kv_jobs + n_chunks, rk, kv_issue, cv_hbm, kvbuf, kvsem):
            et = jnp.exp(att_ref[c] - m).T
            lsum = lsum + jnp.sum(et, axis=1, keepdims=True)
            probs = jnp.concatenate([et[pg * GROUP_LANES:(pg + 1) * GROUP_LANES, :] for pg in range(npc)], axis=1)
            values = jnp.concatenate([kvbuf[slot, pg].astype(BF16) for pg in range(npc)], axis=0)
            acc = acc + _bdot(probs, values)
        return acc, lsum

    acc, lsum = lax.fori_loop(0, n_chunks // DECODE_UNROLL, v_chunks,
                              (jnp.zeros((GROUP_LANES, ATT_HD), F32), jnp.zeros((LANES, 1), F32)))
    top = lambda r: jnp.concatenate([r, jnp.zeros((LANES - r.shape[0], LANES), F32)], axis=0)
    et_new = top(new_pairs * e_new).T
    acc = acc + _bdot(et_new[0:GROUP_LANES, :], top(vn_ref[...]))
    lsum = lsum + jnp.sum(et_new, axis=1, keepdims=True)
    l16 = lsum[0:GROUP_LANES, :]
    for pg in range(1, npc):
        l16 = l16 + lsum[pg * GROUP_LANES:(pg + 1) * GROUP_LANES, :]
    o_ref[...] = (acc / l16).astype(BF16)


def _dsa_step(z2d, cache_k, cache_v, cache_kidx, page_table, *, d_model):
    off = _odd_columns(d_model)
    bsz = z2d.shape[0]
    n_pages = page_table.shape[1]
    n_phys, page, kvh, hd = cache_k.shape
    n_kv = kvh * hd
    heads = d_model // hd
    assert page == PAGE_SIZE and heads == GROUP_LANES and n_pages % PAGES_PER_CHUNK == 0
    n_chunks = n_pages // PAGES_PER_CHUNK
    topk = min(INDEX_TOPK, (n_pages * page + 1) // 4)
    q = z2d[:, off["q"]:off["q"] + d_model].reshape(bsz, kvh, heads // kvh, hd)
    qi = z2d[:, off["qi"]:off["qi"] + IDX_HEADS * IDX_DIM].reshape(bsz, IDX_HEADS, IDX_DIM)
    tile = lambda a: jnp.tile(a, (1, 1, PAGES_PER_CHUNK))
    qit = tile(jnp.transpose(qi, (0, 2, 1)))
    wq = tile(jnp.transpose(q.reshape(bsz, heads, hd), (0, 2, 1)))
    wl = tile(z2d[:, off["wi"]:off["wi"] + IDX_HEADS].reshape(bsz, 1, IDX_HEADS))
    kin = z2d[:, off["ki"]:off["ki"] + IDX_DIM].reshape(bsz, 1, IDX_DIM)
    new_rows = lambda c: jnp.pad(z2d[:, c:c + n_kv].reshape(bsz, kvh, hd), ((0, 0), (0, 8 - kvh), (0, 0)))
    kn, vn = new_rows(off["k"]), new_rows(off["v"])
    expand = jnp.repeat(jnp.eye(page, dtype=BF16), kvh, axis=0)
    per_seq = lambda a: pl.BlockSpec((None,) + a.shape[1:], lambda b, pt: (b, 0, 0))
    hbm = pl.BlockSpec(memory_space=pl.ANY)
    grid_spec = pltpu.PrefetchScalarGridSpec(
        num_scalar_prefetch=1,
        grid=(bsz,),
        in_specs=[per_seq(qit), per_seq(wq), per_seq(wl), per_seq(kin), per_seq(kn), per_seq(vn),
                  pl.BlockSpec(expand.shape, lambda b, pt: (0, 0)), hbm, hbm, hbm],
        out_specs=pl.BlockSpec((None, heads, hd), lambda b, pt: (b, 0, 0)),
        scratch_shapes=[
            pltpu.VMEM((IDX_RING, PAGES_PER_CHUNK, page, IDX_DIM), F32),
            pltpu.VMEM((KV_RING, PAGES_PER_CHUNK, page * kvh, hd), F32),
            pltpu.SemaphoreType.DMA((IDX_RING,)),
            pltpu.SemaphoreType.DMA((KV_RING,)),
            pltpu.VMEM((PAGES_PER_CHUNK * IDX_DIM, LANES), BF16),
            pltpu.VMEM((PAGES_PER_CHUNK * hd, LANES), BF16),
            pltpu.VMEM((n_chunks, page, LANES), jnp.int32),
            pltpu.VMEM((n_chunks, page * kvh, LANES), F32),
        ])
    o = pl.pallas_call(
        functools.partial(_dsa_step_kernel, topk=topk),
        grid_spec=grid_spec,
        out_shape=jax.ShapeDtypeStruct((bsz, heads, hd), BF16),
        compiler_params=_params(("arbitrary",), 40),
        name="dsa_step",
    )(page_table, qit, wq, wl, kin, kn, vn, expand, cache_kidx, cache_k.reshape(n_phys, page * kvh, hd),
      cache_v.reshape(n_phys, page * kvh, hd))
    return o.reshape(bsz, d_model)


def _dsa_prompt(z2d, *, batch, seq, d_model, qb):
    off = _odd_columns(d_model)
    nq = seq // qb
    n_kv = ATT_KV_HEADS * ATT_HD
    topk = min(INDEX_TOPK, seq // 4)
    n_ext = math.gcd(nq, DSA_EXTENTS)
    nqc = nq // n_ext
    outs = []
    for c in range(n_ext):
        q0, nk = c * nqc, (c + 1) * nqc * qb
        rows = lambda w, col, q0=q0: pl.BlockSpec((qb, w), lambda b, i: (b * nq + q0 + i, col // w))
        whole = lambda w, col: pl.BlockSpec((seq, w), lambda b, i: (b, col // w))
        outs.append(pl.pallas_call(
            functools.partial(_dsa_kernel, topk=topk, nk=nk, q0=q0),
            grid=(batch, nqc),
            in_specs=[rows(d_model, off["q"]), rows(IDX_HEADS * IDX_DIM, off["qi"]), rows(LANES, off["wi"]),
                      whole(IDX_DIM, off["ki"]), whole(n_kv, off["k"]), whole(n_kv, off["v"])],
            out_specs=pl.BlockSpec((qb, d_model), lambda b, i: (b * nqc + i, 0)),
            out_shape=jax.ShapeDtypeStruct((batch * nqc * qb, d_model), BF16),
            scratch_shapes=[pltpu.VMEM((nk, IDX_DIM), BF16), pltpu.VMEM((nk, n_kv), BF16), pltpu.VMEM((nk, n_kv), BF16),
                            pltpu.VMEM((qb, nk), jnp.int32), pltpu.VMEM((qb, 1), jnp.int32)],
            compiler_params=_params(("parallel", "arbitrary"), 56),
            name="dsa_prompt",
        )(z2d, z2d, z2d, z2d, z2d, z2d).reshape(batch, nqc * qb, d_model))
    return jnp.concatenate(outs, axis=1).reshape(batch * seq, d_model)


def _trunk(x2d, mods, wts, cfg, even_fn, odd_fn):
    tm, tf, tpb = cfg["tm"], cfg["tf"], cfg["tpb"]
    ffn = functools.partial(_ffn, tm=tm, tf=tf, tiles_per_batch=tpb, vmem_mib=cfg["ffn_vmem"])
    x = x2d
    states = []
    depth = wts["norm_g"].shape[0]
    for layer in range(depth):
        g, mod = wts["norm_g"][layer], mods[layer]
        x = ffn(x, g[0], mod, 0, wts["w_ffn_in"], wts["w_ffn_out"], (layer, 0), wts["final_g"], final_norm=False)
        x, st = (even_fn if layer % 2 == 0 else odd_fn)(x, g[1], mod)
        states.append(st)
        x = ffn(x, g[2], mod, 2, wts["w_ffn_in"], wts["w_ffn_out"], (layer, 1), wts["final_g"],
                final_norm=layer == depth - 1)
    return x, states


def kernel(x_prompt, x_sample, c_prompt, c_sample, state_ret, state_s5_re, state_s5_im, cache_k, cache_v, cache_kidx,
           page_table, norm_g, w_ada, b_ada, w_ffn_in, w_ffn_out, w_in_even, w_out_even, s5_a_re, s5_a_im, s5_log_dt,
           s5_b_re, s5_b_im, s5_c_re, s5_c_im, s5_d, w_glu, b_glu, w_in_odd, w_out_odd, final_g):
    bp, lp, d = x_prompt.shape
    bs, ls, _ = x_sample.shape
    assert ls == 1
    depth = w_ada.shape[0]
    n_mod = N_SUBLAYERS * 3

    rows = -(-(bp + bs) // 8) * 8
    c_all = jnp.concatenate([c_prompt, c_sample, jnp.zeros((rows - bp - bs, d), F32)], axis=0)
    mod_all = _ada_mod(c_all, w_ada, b_ada)
    mods_p = [mod_all[l, :bp].reshape(bp, n_mod, 1, d) for l in range(depth)]
    mods_s = [jnp.transpose(mod_all[l, bp:bp + bs].reshape(bs, n_mod, d), (1, 0, 2))[None] for l in range(depth)]

    p = dict(w_in_even=w_in_even, w_out_even=w_out_even, s5_a_re=s5_a_re, s5_a_im=s5_a_im, s5_log_dt=s5_log_dt,
             s5_b_re=s5_b_re, s5_b_im=s5_b_im, s5_c_re=s5_c_re, s5_c_im=s5_c_im, s5_d=s5_d, w_glu=w_glu, b_glu=b_glu)
    sp = _s5_prepare(p)
    w_odd = _odd_weight(w_in_odd, d)
    off = _odd_columns(d)
    n_kv = ATT_KV_HEADS * ATT_HD
    wts = dict(norm_g=norm_g, w_ffn_in=w_ffn_in, w_ffn_out=w_ffn_out, final_g=final_g)
    odd_tn = 768

    def cache_parts(z, b, l):
        return (z[:, off["k"]:off["k"] + n_kv].reshape(b, l, ATT_KV_HEADS, ATT_HD),
                z[:, off["v"]:off["v"] + n_kv].reshape(b, l, ATT_KV_HEADS, ATT_HD),
                z[:, off["ki"]:off["ki"] + IDX_DIM].reshape(b, l, IDX_DIM))

    cfg_p = dict(tm=min(1024, lp), tf=512, ffn_vmem=62)
    cfg_p["tpb"] = lp // cfg_p["tm"]

    def even_p(x, g, mod):
        zero_s5 = jnp.zeros((bp,) + state_s5_re.shape[1:], F32)
        x, s_ret, sr, si = _even_layer(x, g, mod, p, jnp.zeros((bp,) + state_ret.shape[1:], F32), zero_s5, zero_s5,
                                       batch=bp, seq=lp, pos0=0, tm=cfg_p["tm"], tiles_per_batch=cfg_p["tpb"],
                                       rb=min(256, lp), chunk=min(RET_CHUNK, lp), s5_t=min(128, lp), sp=sp)
        return x, (s_ret, sr, si)

    def odd_p(x, g, mod):
        z = _proj(x, g, mod, 1, w_odd, tm=cfg_p["tm"], tn=odd_tn, tiles_per_batch=cfg_p["tpb"], vmem_mib=52)
        o = _dsa_prompt(z, batch=bp, seq=lp, d_model=d, qb=min(128, lp))
        x = _outproj([(o, d, None)], w_out_odd, x, mod, tm=cfg_p["tm"], tn=1024, tiles_per_batch=cfg_p["tpb"],
                     vmem_mib=48)
        return x, cache_parts(z, bp, lp)

    y_p, (ev_p, od_p) = _trunk(x_prompt.reshape(bp * lp, d), mods_p, wts, cfg_p, even_p, odd_p)

    past_len = page_table.shape[1] * cache_k.shape[1]
    cfg_s = dict(tm=bs, tf=512, tpb=1, ffn_vmem=40)

    def even_s(x, g, mod):
        x, s_ret, sr, si = _even_layer(x, g, mod, p, state_ret, state_s5_re, state_s5_im, batch=bs, seq=1,
                                       pos0=past_len, tm=bs, tiles_per_batch=1, rb=1, chunk=1, s5_t=1, sp=sp)
        return x, (s_ret, sr, si)

    def odd_s(x, g, mod):
        z = _proj(x, g, mod, 1, w_odd, tm=bs, tn=odd_tn, tiles_per_batch=1, vmem_mib=40)
        o = _dsa_step(z, cache_k, cache_v, cache_kidx, page_table, d_model=d)
        x = _outproj([(o, d, None)], w_out_odd, x, mod, tm=bs, tn=1024, tiles_per_batch=1, vmem_mib=40)
        return x, cache_parts(z, bs, 1)

    y_s, (ev_s, od_s) = _trunk(x_sample.reshape(bs, d), mods_s, wts, cfg_s, even_s, odd_s)

    return (y_p.reshape(bp, lp, d), y_s.reshape(bs, 1, d), *ev_p, *od_p, *ev_s, *od_s)
```

```python
import functools
import math

import jax
import jax.numpy as jnp
from jax import lax
from jax.experimental import pallas as pl
from jax.experimental.pallas import tpu as pltpu

F32 = jnp.float32
BF16 = jnp.bfloat16

EPS = 1e-6
FFN_RES = 0.5
N_SUBLAYERS = 3
RET_HEADS = 4
RET_CHUNK = 128
ROPE_BASE = 10000.0
S5_GROUP_CH = 16
S5_STATE = 64
ATT_HD = 128
ATT_KV_HEADS = 4
IDX_HEADS = 16
IDX_DIM = 128
INDEX_TOPK = 256
PAGE_SIZE = 128

LANES = 128
MIB = 1024 * 1024
NEG_INF = float("-inf")


def _params(sem, vmem_mib):
    return pltpu.CompilerParams(dimension_semantics=sem, vmem_limit_bytes=vmem_mib * MIB)


def _bdot(a, b):
    return jnp.dot(a.astype(BF16), b.astype(BF16), preferred_element_type=F32)


def _bdot_nt(a, b):
    return lax.dot_general(a.astype(BF16), b.astype(BF16), (((1,), (1,)), ((), ())),
                           preferred_element_type=F32)


def _bdot_tn(a, b):
    return lax.dot_general(a.astype(BF16), b.astype(BF16), (((0,), (0,)), ((), ())),
                           preferred_element_type=F32)


def _silu(x):
    return x * jax.nn.sigmoid(x)


def _rms(x, g):
    ms = jnp.mean(x * x, axis=-1, keepdims=True)
    return x * lax.rsqrt(ms + EPS) * g


def _row_chunks(n_rows, chunk, fn):
    if n_rows <= chunk:
        fn(slice(0, n_rows))
        return
    assert n_rows % chunk == 0

    def body(i, c):
        fn(pl.ds(pl.multiple_of(i * chunk, chunk), chunk))
        return c

    lax.fori_loop(0, n_rows // chunk, body, 0)


ROW_CHUNK = 256


def _norm_mod_to(h_ref, x_ref, g_ref, sc_ref, sh_ref):
    n_rows = x_ref.shape[0]
    assert sc_ref.shape[0] == 1 or n_rows <= ROW_CHUNK

    def fn(rows):
        y = _rms(x_ref[rows, :], g_ref[...])
        h_ref[rows, :] = (y * (1.0 + sc_ref[...]) + sh_ref[...]).astype(BF16)

    _row_chunks(n_rows, ROW_CHUNK, fn)


def _ada_kernel(c_ref, w_ref, b_ref, o_ref):
    o_ref[...] = _bdot(_silu(c_ref[...]), w_ref[...]) + b_ref[...]


def _ada_mod(c_all, w_ada, b_ada):
    depth, d, n = w_ada.shape
    rows = c_all.shape[0]
    tn = 1024
    return pl.pallas_call(
        _ada_kernel,
        grid=(depth, n // tn),
        in_specs=[
            pl.BlockSpec((rows, d), lambda l, j: (0, 0)),
            pl.BlockSpec((None, d, tn), lambda l, j: (l, 0, j)),
            pl.BlockSpec((None, 1, tn), lambda l, j: (l, 0, j)),
        ],
        out_specs=pl.BlockSpec((None, rows, tn), lambda l, j: (l, 0, j)),
        out_shape=jax.ShapeDtypeStruct((depth, rows, n), F32),
        compiler_params=_params(("arbitrary", "arbitrary"), 40),
        name="ada_mod",
    )(c_all, w_ada, b_ada.reshape(depth, 1, n))


def _mod_specs(mod, sub, tiles_per_batch, which):
    _, _, r, d = mod.shape
    specs = []
    for k in which:
        idx = sub * 3 + k
        specs.append(pl.BlockSpec((None, None, r, d),
                                  lambda i, j, idx=idx: (i // tiles_per_batch, idx, 0, 0)))
    return specs


def _ffn_kernel(x_ref, g_ref, sh_ref, sc_ref, gt_ref, wa_ref, wb_ref, wo_ref, fg_ref, o_ref, h_ref,
                *, final_norm):
    j = pl.program_id(1)
    nj = pl.num_programs(1)

    @pl.when(j == 0)
    def _():
        _norm_mod_to(h_ref, x_ref, g_ref, sc_ref, sh_ref)
        o_ref[...] = jnp.zeros(o_ref.shape, F32)

    h = h_ref[...]
    a = jnp.dot(h, wa_ref[...].astype(BF16), preferred_element_type=F32)
    b = jnp.dot(h, wb_ref[...].astype(BF16), preferred_element_type=F32)
    o_ref[...] += _bdot(_silu(a) * b, wo_ref[...])

    @pl.when(j == nj - 1)
    def _():
        def fn(rows):
            y = x_ref[rows, :] + (FFN_RES * gt_ref[...]) * o_ref[rows, :]
            if final_norm:
                y = _rms(y, fg_ref[...])
            o_ref[rows, :] = y

        _row_chunks(x_ref.shape[0], ROW_CHUNK, fn)


def _ffn(x2d, norm_g, mod, sub, w_in, w_out, wsel, final_g, *, tm, tf, tiles_per_batch, final_norm, vmem_mib):
    m, d = x2d.shape
    f = w_out.shape[2]
    nf = f // tf
    assert m % tm == 0 and f % tf == 0
    row = lambda i, j: (0, 0)
    l, k = wsel
    return pl.pallas_call(
        functools.partial(_ffn_kernel, final_norm=final_norm),
        grid=(m // tm, nf),
        in_specs=[
            pl.BlockSpec((tm, d), lambda i, j: (i, 0)),
            pl.BlockSpec((1, d), row),
            *_mod_specs(mod, sub, tiles_per_batch, (0, 1, 2)),
            pl.BlockSpec((None, None, d, tf), lambda i, j: (l, k, 0, j)),
            pl.BlockSpec((None, None, d, tf), lambda i, j: (l, k, 0, nf + j)),
            pl.BlockSpec((None, None, tf, d), lambda i, j: (l, k, j, 0)),
            pl.BlockSpec((1, d), row),
        ],
        out_specs=pl.BlockSpec((tm, d), lambda i, j: (i, 0)),
        out_shape=jax.ShapeDtypeStruct((m, d), F32),
        scratch_shapes=[pltpu.VMEM((tm, d), BF16)],
        compiler_params=_params(("parallel", "arbitrary"), vmem_mib),
        name="ffn",
    )(x2d, norm_g.reshape(1, d), mod, mod, mod, w_in, w_in, w_out, final_g.reshape(1, d))


def _proj_kernel(x_ref, g_ref, sh_ref, sc_ref, w_ref, o_ref, h_ref):
    @pl.when(pl.program_id(1) == 0)
    def _():
        _norm_mod_to(h_ref, x_ref, g_ref, sc_ref, sh_ref)

    o_ref[...] = jnp.dot(h_ref[...], w_ref[...].astype(BF16), preferred_element_type=F32)


def _proj(x2d, norm_g, mod, sub, w, *, tm, tn, tiles_per_batch, vmem_mib):
    m, d = x2d.shape
    n = w.shape[1]
    assert m % tm == 0 and n % tn == 0
    return pl.pallas_call(
        _proj_kernel,
        grid=(m // tm, n // tn),
        in_specs=[
            pl.BlockSpec((tm, d), lambda i, j: (i, 0)),
            pl.BlockSpec((1, d), lambda i, j: (0, 0)),
            *_mod_specs(mod, sub, tiles_per_batch, (0, 1)),
            pl.BlockSpec((d, tn), lambda i, j: (0, j)),
        ],
        out_specs=pl.BlockSpec((tm, tn), lambda i, j: (i, j)),
        out_shape=jax.ShapeDtypeStruct((m, n), F32),
        scratch_shapes=[pltpu.VMEM((tm, d), BF16)],
        compiler_params=_params(("parallel", "arbitrary"), vmem_mib),
        name="mixer_in_proj",
    )(x2d, norm_g.reshape(1, d), mod, mod, w)


def _outproj_kernel(*refs, n_parts):
    a_refs = refs[:n_parts]
    w_ref, x_ref, gt_ref, o_ref = refs[n_parts:]
    acc = None
    k0 = 0
    for a_ref in a_refs:
        kk = a_ref.shape[1]
        p = jnp.dot(a_ref[...], w_ref[k0:k0 + kk, :].astype(BF16), preferred_element_type=F32)
        acc = p if acc is None else acc + p
        k0 += kk
    o_ref[...] = x_ref[...] + gt_ref[...] * acc


def _outproj(a_parts, w, x2d, mod, *, tm, tn, tiles_per_batch, vmem_mib):
    m, d = x2d.shape
    k = w.shape[0]
    assert sum(kw for _, kw, _ in a_parts) == k and m % tm == 0 and d % tn == 0
    _, _, r, _ = mod.shape
    return pl.pallas_call(
        functools.partial(_outproj_kernel, n_parts=len(a_parts)),
        grid=(m // tm, d // tn),
        in_specs=[
            *[pl.BlockSpec((tm, kw), imap or (lambda i, j: (i, 0))) for _, kw, imap in a_parts],
            pl.BlockSpec((k, tn), lambda i, j: (0, j)),
            pl.BlockSpec((tm, tn), lambda i, j: (i, j)),
            pl.BlockSpec((None, None, r, tn), lambda i, j: (i // tiles_per_batch, 5, 0, j)),
        ],
        out_specs=pl.BlockSpec((tm, tn), lambda i, j: (i, j)),
        out_shape=jax.ShapeDtypeStruct((m, d), F32),
        compiler_params=_params(("parallel", "arbitrary"), vmem_mib),
        name="mixer_out_proj",
    )(*[a for a, _, _ in a_parts], w, x2d, mod)


def _rot(x1, x2, cos, sin):
    return jnp.concatenate([x1 * cos - x2 * sin, x1 * sin + x2 * cos], axis=-1)


def _head_norm(o):
    mu = jnp.mean(o, axis=-1, keepdims=True)
    var = jnp.mean(jnp.square(o - mu), axis=-1, keepdims=True)
    return (o - mu) * lax.rsqrt(var + 1e-5)


def _ret_log_g(h):
    return math.log1p(-(2.0 ** (-5.0 - h)))


def _ret_kernel(q_ref, k_ref, v_ref, g_ref, cos_ref, sin_ref, s0_ref, o_ref, s_ref, *, chunk):
    c = pl.program_id(1)
    rb = q_ref.shape[0]
    dk = q_ref.shape[1] // RET_HEADS
    half = dk // 2

    @pl.when(c == 0)
    def _():
        s_ref[...] = s0_ref[...]

    ri = lax.broadcasted_iota(jnp.int32, (chunk, chunk), 0).astype(F32)
    ci = lax.broadcasted_iota(jnp.int32, (chunk, chunk), 1).astype(F32)
    diff = ri - ci
    rowf = lax.broadcasted_iota(jnp.int32, (chunk, dk), 0).astype(F32)
    for h in range(RET_HEADS):
        lg = _ret_log_g(h)
        dmat = jnp.where(diff >= 0, jnp.exp(jnp.maximum(diff, 0.0) * lg), 0.0)
        q_dec = jnp.exp((rowf + 1.0) * lg)
        k_dec = jnp.exp((chunk - 1.0 - rowf) * lg)
        c_dec = math.exp(chunk * lg)
        c0 = h * dk
        for ck in range(rb // chunk):
            rows = slice(ck * chunk, (ck + 1) * chunk)
            cos = cos_ref[rows, :]
            sin = sin_ref[rows, :]
            qr = _rot(q_ref[rows, c0:c0 + half], q_ref[rows, c0 + half:c0 + dk], cos, sin)
            kr = _rot(k_ref[rows, c0:c0 + half], k_ref[rows, c0 + half:c0 + dk], cos, sin) * (dk ** -0.5)
            vb = v_ref[rows, c0:c0 + dk].astype(BF16)
            qb = qr.astype(BF16)
            a = _bdot_nt(qb, kr) * dmat
            s = s_ref[h]
            o = _bdot(a, vb) + _bdot(qb, s) * q_dec
            s_ref[h] = s * c_dec + _bdot_tn(kr * k_dec, vb)
            o_ref[rows, c0:c0 + dk] = (_head_norm(o) * _silu(g_ref[rows, c0:c0 + dk])).astype(BF16)


def _retention(z2d, col0, cos, sin, state0, *, batch, seq, rb, chunk):
    _, h, dk, dv = state0.shape
    w = h * dk
    nb = seq // rb
    assert seq % rb == 0 and rb % chunk == 0 and col0 % w == 0
    cb = col0 // w
    zspec = lambda k: pl.BlockSpec((rb, w), lambda b, c, k=k: (b * nb + c, cb + k))
    tab = pl.BlockSpec((rb, dk // 2), lambda b, c: (c, 0))
    st = pl.BlockSpec((None, h, dk, dv), lambda b, c: (b, 0, 0, 0))
    return pl.pallas_call(
        functools.partial(_ret_kernel, chunk=chunk),
        grid=(batch, nb),
        in_specs=[zspec(0), zspec(1), zspec(2), zspec(3), tab, tab, st],
        out_specs=[pl.BlockSpec((rb, w), lambda b, c: (b * nb + c, 0)), st],
        out_shape=[jax.ShapeDtypeStruct((batch * seq, w), BF16),
                   jax.ShapeDtypeStruct(state0.shape, F32)],
        compiler_params=_params(("parallel", "arbitrary"), 40),
        name="retention",
    )(z2d, z2d, z2d, z2d, cos, sin, state0)


def _rope_tables(pos, half):
    freqs = ROPE_BASE ** (-jnp.arange(half, dtype=F32) / half)
    ang = pos.astype(F32)[:, None] * freqs[None, :]
    return jnp.cos(ang), jnp.sin(ang)


def _r16(x):
    return x.astype(BF16).astype(F32)


def _ret_step_kernel(q_ref, krow_ref, kcol_ref, v_ref, g_ref, cos_ref, sin_ref, cosc_ref, sinc_ref, s0_ref,
                     o_ref, s_ref):
    dk = q_ref.shape[1] // RET_HEADS
    half = dk // 2
    cos, sin = cos_ref[...], sin_ref[...]
    cosc, sinc = cosc_ref[...], sinc_ref[...]
    for h in range(RET_HEADS):
        dec = math.exp(_ret_log_g(h))
        c0 = h * dk
        qr = _r16(_rot(q_ref[:, c0:c0 + half], q_ref[:, c0 + half:c0 + dk], cos, sin))
        kr = _r16(_rot(krow_ref[:, c0:c0 + half], krow_ref[:, c0 + half:c0 + dk], cos, sin) * (dk ** -0.5))
        k1, k2 = kcol_ref[h, 0:half, :], kcol_ref[h, half:dk, :]
        kc = _r16(jnp.concatenate([k1 * cosc - k2 * sinc, k1 * sinc + k2 * cosc], axis=0) * (dk ** -0.5))
        v = _r16(v_ref[:, c0:c0 + dk])
        a = jnp.sum(qr * kr, axis=-1, keepdims=True)
        s = s0_ref[h]
        o = _r16(a) * v + _bdot(qr, s) * dec
        s_ref[h] = s * dec + kc * v
        o_ref[:, c0:c0 + dk] = (_head_norm(o) * _silu(g_ref[:, c0:c0 + dk])).astype(BF16)


def _retention_step(z2d, cos, sin, state0):
    b, h, dk, dv = state0.shape
    w = h * dk
    z3 = z2d.reshape(b, 1, z2d.shape[1])
    kcol = z2d[:, w:2 * w].reshape(b, h, dk, 1)
    half = dk // 2
    zspec = lambda k: pl.BlockSpec((None, 1, w), lambda i, k=k: (i, 0, k))
    row = pl.BlockSpec((1, half), lambda i: (0, 0))
    col = pl.BlockSpec((half, 1), lambda i: (0, 0))
    st = pl.BlockSpec((None, h, dk, dv), lambda i: (i, 0, 0, 0))
    og, s = pl.pallas_call(
        _ret_step_kernel,
        grid=(b,),
        in_specs=[zspec(0), zspec(1), pl.BlockSpec((None, h, dk, 1), lambda i: (i, 0, 0, 0)), zspec(2), zspec(3),
                  row, row, col, col, st],
        out_specs=[pl.BlockSpec((None, 1, w), lambda i: (i, 0, 0)), st],
        out_shape=[jax.ShapeDtypeStruct((b, 1, w), BF16), jax.ShapeDtypeStruct(state0.shape, F32)],
        compiler_params=_params(("parallel",), 32),
        name="retention_step",
    )(z3, z3, kcol, z3, z3, cos, sin, cos.reshape(half, 1), sin.reshape(half, 1), state0)
    return og.reshape(b, w), s


S5_BLOCKS = 8


def _s5_disc(ar, ai, dt):
    mag = jnp.exp(dt * ar)
    abr = mag * jnp.cos(dt * ai)
    abi = mag * jnp.sin(dt * ai)
    den = ar * ar + ai * ai
    xr = abr - 1.0
    return abr, abi, (xr * ar + abi * ai) / den, (abi * ar - xr * ai) / den


def _s5_param_kernel(ar_ref, ai_ref, ldt_ref, arr_ref, air_ref, br_ref, bi_ref, abr_ref, abi_ref, bbr_ref, bbi_ref):
    dt = jnp.exp(ldt_ref[...])
    abr_ref[...], abi_ref[...], _, _ = _s5_disc(ar_ref[...], ai_ref[...], dt)
    _, _, fr, fi = _s5_disc(arr_ref[...], air_ref[...], dt)
    br, bi = br_ref[...], bi_ref[...]
    bbr_ref[...] = fr * br - fi * bi
    bbi_ref[...] = fr * bi + fi * br


def _s5_prepare(p):
    g, st = p["s5_a_re"].shape
    ch = p["s5_d"].shape[1]
    rep = lambda a: jnp.repeat(a, ch, axis=1)
    abr, abi, bbr, bbi = pl.pallas_call(
        _s5_param_kernel,
        out_shape=[jax.ShapeDtypeStruct((g, st), F32)] * 2 + [jax.ShapeDtypeStruct((g, st * ch), F32)] * 2,
        name="s5_discretise",
    )(p["s5_a_re"], p["s5_a_im"], p["s5_log_dt"].reshape(g, 1), rep(p["s5_a_re"]), rep(p["s5_a_im"]),
      p["s5_b_re"].reshape(g, st * ch), p["s5_b_im"].reshape(g, st * ch))
    nb = S5_BLOCKS
    gl = g // nb
    eye = jnp.eye(gl, dtype=F32)
    w_in = lambda bb: jnp.einsum("bgpc,gh->bgchp", bb.reshape(nb, gl, st, ch), eye).reshape(nb, gl * ch, gl * st)
    w_out = lambda c: jnp.einsum("bgcp,gh->bgphc", c.reshape(nb, gl, ch, st), eye).reshape(nb, gl * st, gl * ch)
    return dict(
        abr=abr.reshape(1, g * st), abi=abi.reshape(1, g * st),
        wbr=w_in(bbr).astype(BF16), wbi=w_in(bbi).astype(BF16),
        wcr=w_out(p["s5_c_re"]).astype(BF16), wci=w_out(p["s5_c_im"]).astype(BF16),
        d=p["s5_d"].reshape(1, g * ch), w_glu=p["w_glu"], b_glu=p["b_glu"].reshape(1, -1))


S5_SCAN_LANES = 1024


def _s5_kernel(u_ref, x0r_ref, x0i_ref, abr_ref, abi_ref, wbr_ref, wbi_ref, wcr_ref, wci_ref, d_ref, wg_ref,
               bg_ref, y_ref, xr_ref, xi_ref, bur, bui, yb, wgb, *, ns, t):
    nblk = wbr_ref.shape[0]
    uw, sw = wbr_ref.shape[1], wbr_ref.shape[2]
    n_tiles = nblk * sw // LANES
    sub = 8

    @pl.when(pl.program_id(0) == 0)
    def _():
        xr_ref[...] = x0r_ref[...]
        xi_ref[...] = x0i_ref[...]
        wgb[...] = wg_ref[...].astype(BF16)

    tiles_per_blk = sw // LANES
    lane_tile = lambda j: slice(j * LANES, (j + 1) * LANES)
    for b in range(nblk):
        ub = u_ref[:, b * uw:(b + 1) * uw].astype(BF16)
        pr = jnp.dot(ub, wbr_ref[b], preferred_element_type=F32)
        pi = jnp.dot(ub, wbi_ref[b], preferred_element_type=F32)
        for k in range(tiles_per_blk):
            bur[b * tiles_per_blk + k] = pr[:, lane_tile(k)]
            bui[b * tiles_per_blk + k] = pi[:, lane_tile(k)]

    def affine(ar, ai, xr, xi, vr, vi):
        return ar * xr - ai * xi + vr, ar * xi + ai * xr + vi

    per_pass = S5_SCAN_LANES // LANES
    for j0 in range(0, n_tiles, per_pass):
        tiles = list(range(j0, min(j0 + per_pass, n_tiles)))
        if t == 1:
            for j in tiles:
                ar = jnp.broadcast_to(abr_ref[:, lane_tile(j)], (ns, LANES))
                ai = jnp.broadcast_to(abi_ref[:, lane_tile(j)], (ns, LANES))
                nr, ni = affine(ar, ai, xr_ref[:, lane_tile(j)], xi_ref[:, lane_tile(j)], bur[j], bui[j])
                bur[j], bui[j] = nr, ni
                xr_ref[:, lane_tile(j)], xi_ref[:, lane_tile(j)] = nr, ni
            continue
        assert 2 * ns == sub and t % 2 == 0
        ar = [jnp.broadcast_to(abr_ref[:, lane_tile(j)], (sub, LANES)) for j in tiles]
        ai = [jnp.broadcast_to(abi_ref[:, lane_tile(j)], (sub, LANES)) for j in tiles]
        top = lax.broadcasted_iota(jnp.int32, (sub, LANES), 0) < ns
        swap = lambda x: pltpu.roll(x, ns, axis=0)

        def step(i, carry, tiles=tiles, ar=ar, ai=ai):
            rows = pl.ds(pl.multiple_of(i * sub, sub), sub)
            out = []
            for k, j in enumerate(tiles):
                pr_, pi_ = carry[k]
                vr, vi = bur[j, rows, :], bui[j, rows, :]
                er, ei = affine(ar[k], ai[k], swap(pr_), swap(pi_), vr, vi)
                orr, oi = affine(ar[k], ai[k], swap(er), swap(ei), vr, vi)
                nr, ni = jnp.where(top, er, orr), jnp.where(top, ei, oi)
                bur[j, rows, :] = nr
                bui[j, rows, :] = ni
                out.append((nr, ni))
            return tuple(out)

        twice = lambda x: jnp.concatenate([x, x], axis=0)
        fin = lax.fori_loop(0, t // 2, step,
                            tuple((twice(xr_ref[:, lane_tile(j)]), twice(xi_ref[:, lane_tile(j)])) for j in tiles))
        for k, j in enumerate(tiles):
            xr_ref[:, lane_tile(j)] = fin[k][0][ns:, :]
            xi_ref[:, lane_tile(j)] = fin[k][1][ns:, :]

    blk_states = lambda ref, b: jnp.concatenate(
        [ref[b * tiles_per_blk + k] for k in range(tiles_per_blk)], axis=1).astype(BF16)
    for b in range(nblk):
        yb[:, b * uw:(b + 1) * uw] = (jnp.dot(blk_states(bur, b), wcr_ref[b], preferred_element_type=F32)
                                      - jnp.dot(blk_states(bui, b), wci_ref[b], preferred_element_type=F32))
    gl = jax.nn.gelu(yb[...] + d_ref[...] * u_ref[...])
    gate = jax.nn.sigmoid(jnp.dot(gl.astype(BF16), wgb[...], preferred_element_type=F32) + bg_ref[...])
    y_ref[...] = (gl * gate).astype(BF16)


def _s5(u2d, col_blk, sp, x0r, x0i, *, batch, t):
    w = sp["d"].shape[1]
    n_state = sp["abr"].shape[1]
    rows = batch * t
    assert u2d.shape[0] % rows == 0
    full = lambda a: pl.BlockSpec(a.shape, lambda c, nd=a.ndim: (0,) * nd)
    st_spec = pl.BlockSpec((batch, n_state), lambda c: (0, 0))
    consts = [sp["abr"], sp["abi"], sp["wbr"], sp["wbi"], sp["wcr"], sp["wci"], sp["d"], sp["w_glu"], sp["b_glu"]]
    return pl.pallas_call(
        functools.partial(_s5_kernel, ns=batch, t=t),
        grid=(u2d.shape[0] // rows,),
        in_specs=[pl.BlockSpec((rows, w), lambda c: (c, col_blk)), st_spec, st_spec] + [full(a) for a in consts],
        out_specs=[pl.BlockSpec((rows, w), lambda c: (c, 0)), st_spec, st_spec],
        out_shape=[jax.ShapeDtypeStruct((u2d.shape[0], w), BF16), jax.ShapeDtypeStruct((batch, n_state), F32),
                   jax.ShapeDtypeStruct((batch, n_state), F32)],
        scratch_shapes=[pltpu.VMEM((n_state // LANES, rows, LANES), F32), pltpu.VMEM((n_state // LANES, rows, LANES), F32),
                        pltpu.VMEM((rows, w), F32), pltpu.VMEM(sp["w_glu"].shape, BF16)],
        compiler_params=_params(("arbitrary",), 48),
        name="s5",
    )(u2d, x0r.reshape(batch, n_state), x0i.reshape(batch, n_state), *consts)


def _even_layer(x2d, norm_g, mod, p, ret0, s5r0, s5i0, *, batch, seq, pos0, tm, tiles_per_batch, rb, chunk, s5_t,
                sp=None):
    _, h, dk, _ = ret0.shape
    rw = h * dk
    z = _proj(x2d, norm_g, mod, 1, p["w_in_even"], tm=tm, tn=1024, tiles_per_batch=tiles_per_batch, vmem_mib=52)
    cos, sin = _rope_tables(pos0 + jnp.arange(seq), dk // 2)
    if seq > 1:
        og, s_ret = _retention(z, 0, cos, sin, ret0, batch=batch, seq=seq, rb=rb, chunk=chunk)
    else:
        og, s_ret = _retention_step(z, cos, sin, ret0)
    sp = _s5_prepare(p) if sp is None else sp
    sw = sp["d"].shape[1]
    if seq > 1:
        u2d = jnp.transpose(z.reshape(batch, seq, -1)[:, :, 4 * rw:4 * rw + sw], (1, 0, 2)).reshape(seq * batch, sw)
        y, sr, si = _s5(u2d, 0, sp, s5r0, s5i0, batch=batch, t=s5_t)
        y_part = (y.reshape(seq, batch * sw), sw, lambda i, j: (i % tiles_per_batch, i // tiles_per_batch))
    else:
        y, sr, si = _s5(z, (4 * rw) // sw, sp, s5r0, s5i0, batch=batch, t=1)
        y_part = (y, sw, None)
    x_new = _outproj([(og, rw, None), y_part], p["w_out_even"], x2d, mod, tm=tm, tn=1024,
                     tiles_per_batch=tiles_per_batch, vmem_mib=48)
    return x_new, s_ret, sr.reshape(s5r0.shape), si.reshape(s5i0.shape)


INT32_MIN = -2 ** 31


def _sortable_keys(score):
    score = jnp.where(score == 0.0, 0.0, score)
    bits = lax.bitcast_convert_type(score, jnp.int32)
    return bits ^ (lax.shift_right_arithmetic(bits, 31) & 0x7FFFFFFF)


def _kth_largest(count_ge, shape, k):
    kf = jnp.float32(k)
    prefix = jnp.where(count_ge(jnp.zeros(shape, jnp.int32)) >= kf, 0, INT32_MIN).astype(jnp.int32)

    def body(i, prefix):
        cand = prefix | lax.shift_left(jnp.int32(1), 30 - i)
        return jnp.where(count_ge(cand) >= kf, cand, prefix)

    return lax.fori_loop(0, 31, body, prefix)


def _kth_largest_wide(count_ge, k, digit_bits=4):
    kf = jnp.float32(k)
    prefix = jnp.where(count_ge(jnp.zeros((1, 1), jnp.int32)) >= kf, 0, INT32_MIN).astype(jnp.int32)
    hi = 31
    while hi > 0:
        lo = max(hi - digit_bits, 0)
        digit = jnp.zeros((1, 1), jnp.int32)
        for d in range(1, 1 << (hi - lo)):
            digit = digit + jnp.where(count_ge(prefix | (d << lo)) >= kf, 1, 0)
        prefix = prefix | (digit * (1 << lo))
        hi = lo
    return prefix


def _odd_columns(d_model):
    n_q = d_model
    n_kv = ATT_KV_HEADS * ATT_HD
    n_qi = IDX_HEADS * IDX_DIM
    off = dict(q=0, qi=n_q, k=n_q + n_qi, v=n_q + n_qi + n_kv, ki=n_q + n_qi + 2 * n_kv)
    off["wi"] = off["ki"] + IDX_DIM
    off["end"] = off["wi"] + LANES
    return off


def _odd_weight(w_in_odd, d_model):
    n_kv = ATT_KV_HEADS * ATT_HD
    n_qi = IDX_HEADS * IDX_DIM
    cuts = [d_model, d_model + n_kv, d_model + 2 * n_kv, d_model + 2 * n_kv + n_qi, d_model + 2 * n_kv + n_qi + IDX_DIM]
    q, k, v, qi, ki, wi = jnp.split(w_in_odd, cuts, axis=1)
    pad = jnp.zeros((w_in_odd.shape[0], LANES - wi.shape[1]), w_in_odd.dtype)
    return jnp.concatenate([q, qi, k, v, ki, wi, pad], axis=1)


DSA_EXTENTS = 4


def _dsa_kernel(q_ref, qi_ref, wi_ref, ki_ref, k_ref, v_ref, o_ref, kib, kb, vb, key_ref, cut_ref, *, topk, nk, q0):
    i = pl.program_id(1)
    qb = q_ref.shape[0]
    group = q_ref.shape[1] // ATT_HD // ATT_KV_HEADS

    @pl.when(i == 0)
    def _():
        kib[...] = ki_ref[0:nk, :].astype(BF16)
        kb[...] = k_ref[0:nk, :].astype(BF16)
        vb[...] = v_ref[0:nk, :].astype(BF16)

    score = jnp.zeros((qb, nk), F32)
    for h in range(IDX_HEADS):
        s = _bdot_nt(qi_ref[:, h * IDX_DIM:(h + 1) * IDX_DIM], kib[...]) * (IDX_DIM ** -0.5)
        score = score + jnp.maximum(s, 0.0) * wi_ref[:, h:h + 1]
    score = score * (IDX_HEADS ** -0.5)
    q_pos = (q0 + i) * qb + lax.broadcasted_iota(jnp.int32, (qb, nk), 0)
    col = lax.broadcasted_iota(jnp.int32, (qb, nk), 1)
    key_ref[...] = _sortable_keys(jnp.where(col <= q_pos, score, NEG_INF))

    count = lambda hit: jnp.sum(hit, axis=-1, keepdims=True)
    one = lambda cond: jnp.where(cond, 1.0, 0.0)
    thr = _kth_largest(lambda t: count(one(key_ref[...] >= t)), (qb, 1), topk)

    cut_ref[...] = jnp.full((qb, 1), nk, jnp.int32)

    @pl.when(jnp.max(count(one(key_ref[...] >= thr))) > topk)
    def _():
        need = topk - count(one(key_ref[...] > thr))

        def bit(j, cut):
            cand = cut + lax.shift_left(jnp.int32(1), nk.bit_length() - 1 - j)
            below = count(jnp.where(key_ref[...] == thr, one(col < cand), 0.0))
            return jnp.where(below < need, cand, cut)

        cut_ref[...] = lax.fori_loop(0, nk.bit_length(), bit, jnp.zeros((qb, 1), jnp.int32))

    kt = key_ref[...]
    keep = jnp.where(kt > thr, 0.0, jnp.where(kt == thr, jnp.where(col <= cut_ref[...], 0.0, NEG_INF), NEG_INF))
    bias = jnp.concatenate([jnp.where(col <= q_pos, keep, NEG_INF)] * group, axis=0)
    for n in range(ATT_KV_HEADS):
        heads = range(n * group, (n + 1) * group)
        qs = jnp.concatenate([q_ref[:, h * ATT_HD:(h + 1) * ATT_HD] for h in heads], axis=0)
        s = _bdot_nt(qs, kb[:, n * ATT_HD:(n + 1) * ATT_HD]) * (ATT_HD ** -0.5) + bias
        e = jnp.exp(s - jnp.max(s, axis=-1, keepdims=True))
        o = _bdot(e, vb[:, n * ATT_HD:(n + 1) * ATT_HD]) / jnp.sum(e, axis=-1, keepdims=True)
        for g, h in enumerate(heads):
            o_ref[:, h * ATT_HD:(h + 1) * ATT_HD] = o[g * qb:(g + 1) * qb, :].astype(BF16)


PAGES_PER_CHUNK = 8
GROUP_LANES = LANES // PAGES_PER_CHUNK
IDX_RING = 4
KV_RING = 8
DECODE_UNROLL = 2


def _group_sum(x):
    sh = 1
    while sh < GROUP_LANES:
        x = x + pltpu.roll(x, sh, axis=1)
        sh *= 2
    return x


def _group_spread(x):
    sh = 1
    while sh < GROUP_LANES:
        x = jnp.maximum(x, pltpu.roll(x, LANES - sh, axis=1))
        sh *= 2
    return x


def _across_groups(x, op):
    sh = GROUP_LANES
    while sh < LANES:
        x = op(x, pltpu.roll(x, sh, axis=1))
        sh *= 2
    return x


def _dsa_step_kernel(pt_ref, qit_ref, wq_ref, wl_ref, kin_ref, kn_ref, vn_ref, ex_ref, ci_hbm, ck_hbm, cv_hbm, o_ref,
                     ibuf, kvbuf, isem, kvsem, qit_s, wq_s, key_ref, att_ref, cut_ref, *, topk):
    b = pl.program_id(0)
    nb = pl.num_programs(0)
    npc = PAGES_PER_CHUNK
    n_chunks = key_ref.shape[0]
    assert GROUP_LANES == IDX_HEADS and n_chunks % DECODE_UNROLL == 0 and min(ibuf.shape[0], kvbuf.shape[0]) > DECODE_UNROLL

    def page_copy(src, buf, sem, seq, chunk, slot, pg):
        return pltpu.make_async_copy(src.at[pt_ref[seq, chunk * npc + pg]], buf.at[slot, pg], sem.at[slot])

    def start(src, buf, sem, seq, chunk, slot):
        for pg in range(npc):
            page_copy(src, buf, sem, seq, chunk, slot, pg).start()

    def wait(src, buf, sem, slot):
        for pg in range(npc):
            page_copy(src, buf, sem, 0, 0, slot, pg).wait()

    ri, rk = ibuf.shape[0], kvbuf.shape[0]
    kv_jobs = 2 * n_chunks

    def idx_issue(job):
        @pl.when(job < nb * n_chunks)
        def _():
            start(ci_hbm, ibuf, isem, job // n_chunks, job % n_chunks, job % ri)

    def kv_issue(job):
        seq, jj, slot = job // kv_jobs, job % kv_jobs, job % rk

        @pl.when(jnp.logical_and(job < nb * kv_jobs, jj < n_chunks))
        def _():
            start(ck_hbm, kvbuf, kvsem, seq, jj, slot)

        @pl.when(jnp.logical_and(job < nb * kv_jobs, jj >= n_chunks))
        def _():
            start(cv_hbm, kvbuf, kvsem, seq, jj - n_chunks, slot)

    @pl.when(b == 0)
    def _():
        for job in range(ri - DECODE_UNROLL):
            idx_issue(jnp.int32(job))
        for job in range(rk - DECODE_UNROLL):
            kv_issue(jnp.int32(job))

    def stream_group(g, first_job, ring, issue, src, buf, sem):
        jobs = [first_job + g * DECODE_UNROLL + k for k in range(DECODE_UNROLL)]
        for job in jobs:
            issue(job + ring - DECODE_UNROLL)
        for job in jobs:
            wait(src, buf, sem, job % ring)
        return [(g * DECODE_UNROLL + k, job % ring) for k, job in enumerate(jobs)]

    lane = lax.broadcasted_iota(jnp.int32, (1, LANES), 1)
    lane_grp = lax.shift_right_logical(lane, int(math.log2(GROUP_LANES)))
    last_in_grp = (lane & (GROUP_LANES - 1)) == GROUP_LANES - 1
    for pg in range(npc):
        qit_s[pg * IDX_DIM:(pg + 1) * IDX_DIM, :] = jnp.where(lane_grp == pg, qit_ref[...], 0.0).astype(BF16)
        wq_s[pg * ATT_HD:(pg + 1) * ATT_HD, :] = jnp.where(lane_grp == pg, wq_ref[...], 0.0).astype(BF16)
    w_lane = wl_ref[...]
    kv_rows = kvbuf.shape[2]
    q_per_kv = GROUP_LANES // ATT_KV_HEADS
    kv_of_lane = (lane & (GROUP_LANES - 1)) // q_per_kv
    pairs = lambda n: (lax.broadcasted_iota(jnp.int32, (n, LANES), 0) & (ATT_KV_HEADS - 1)) == kv_of_lane
    first_rows = lambda n: lax.broadcasted_iota(jnp.int32, (n, LANES), 0) < ATT_KV_HEADS

    def idx_score(raw):
        s = jnp.maximum(raw * (IDX_DIM ** -0.5), 0.0) * w_lane
        return _group_sum(s) * (IDX_HEADS ** -0.5)

    def idx_chunks(g, carry):
        for c, slot in stream_group(g, b * n_chunks, ri, idx_issue, ci_hbm, ibuf, isem):
            pages = jnp.concatenate([ibuf[slot, pg].astype(BF16) for pg in range(npc)], axis=1)
            acc = jnp.dot(pages, qit_s[...], preferred_element_type=F32)
            key_ref[c] = _sortable_keys(jnp.where(last_in_grp, idx_score(acc), NEG_INF))
        return carry

    lax.fori_loop(0, n_chunks // DECODE_UNROLL, idx_chunks, 0)
    rows8 = lambda r: jnp.broadcast_to(r, (8, r.shape[1]))
    new_raw = _bdot(rows8(kin_ref[...]), qit_ref[...])[0:1, :]
    key_new = _sortable_keys(jnp.where(lane == GROUP_LANES - 1, idx_score(new_raw), NEG_INF))

    assert n_chunks <= GROUP_LANES
    dense = key_ref[0]
    for c in range(1, n_chunks):
        dense = jnp.maximum(dense, pltpu.roll(key_ref[c], LANES - c, axis=1))

    def count_ge(t):
        tot = jnp.sum(jnp.where(dense >= t, 1.0, 0.0), axis=0, keepdims=True) + jnp.where(key_new >= t, 1.0, 0.0)
        return jnp.sum(tot, axis=1, keepdims=True)

    thr = _kth_largest_wide(count_ge, topk)

    n_pos = n_chunks * npc * PAGE_SIZE
    row_id = lax.broadcasted_iota(jnp.int32, (PAGE_SIZE, LANES), 0)
    cut_ref[...] = jnp.full(cut_ref.shape, 2 ** n_pos.bit_length() - 1, jnp.int32)

    @pl.when(jnp.max(count_ge(thr)) > topk)
    def _():
        total = lambda x: jnp.sum(jnp.sum(x, axis=0, keepdims=True), axis=1, keepdims=True)
        need = topk - total(jnp.where(dense > thr, 1.0, 0.0)) - total(jnp.where(key_new > thr, 1.0, 0.0))
        chunk_of_lane = GROUP_LANES - 1 - (lane & (GROUP_LANES - 1))
        pos = (chunk_of_lane * npc + lane_grp) * PAGE_SIZE + row_id
        cut = jnp.zeros((1, 1), jnp.int32)
        for bit in range(n_pos.bit_length() - 1, -1, -1):
            cand = cut + (1 << bit)
            below = total(jnp.where(dense == thr, jnp.where(pos < cand, 1.0, 0.0), 0.0))
            cut = jnp.where(below < need, cand, cut)
        cut_ref[...] = jnp.broadcast_to(cut, cut_ref.shape)

    cut = cut_ref[0:1, 0:1]
    kept = lambda keys, positions: jnp.where(keys > thr, 1.0, jnp.where(keys == thr,
                                                                       jnp.where(positions <= cut, 1.0, 0.0), 0.0))
    new_sel = jnp.max(kept(key_new, n_pos), axis=1, keepdims=True) > 0.0

    def k_chunks(g, carry):
        for c, slot in stream_group(g, b * kv_jobs, rk, kv_issue, ck_hbm, kvbuf, kvsem):
            pages = jnp.concatenate([kvbuf[slot, pg].astype(BF16) for pg in range(npc)], axis=1)
            acc = jnp.dot(pages, wq_s[...], preferred_element_type=F32)
            sel = _group_spread(kept(key_ref[c], (c * npc + lane_grp) * PAGE_SIZE + row_id))
            sel_rows = jnp.dot(ex_ref[...], sel.astype(BF16), preferred_element_type=F32)
            att_ref[c] = jnp.where(pairs(kv_rows), jnp.where(sel_rows > 0.0, acc * (ATT_HD ** -0.5), NEG_INF),
                                   NEG_INF)
        return carry

    lax.fori_loop(0, n_chunks // DECODE_UNROLL, k_chunks, 0)
    new_pairs = jnp.where(first_rows(8), jnp.where(pairs(8), 1.0, 0.0), 0.0)
    s_new = jnp.sum(new_pairs * _bdot(kn_ref[...], wq_ref[...]), axis=0, keepdims=True)
    s_new = jnp.where(new_sel, s_new * (ATT_HD ** -0.5), NEG_INF)
    m = jnp.max(jnp.max(att_ref[...], axis=0), axis=0, keepdims=True)
    m = jnp.maximum(_across_groups(m, jnp.maximum), s_new)
    e_new = jnp.where(lane < GROUP_LANES, jnp.exp(s_new - m), 0.0)

    def v_chunks(g, carry):
        acc, lsum = carry
        for c, slot in stream_group(g, b * kv_jobs + n_chunks, rk, kv_issue, cv_hbm, kvbuf, kvsem):
            et = jnp.exp(att_ref[c] - m).T
            lsum = lsum + jnp.sum(et, axis=1, keepdims=True)
            probs = jnp.concatenate([et[pg * GROUP_LANES:(pg + 1) * GROUP_LANES, :] for pg in range(npc)], axis=1)
            values = jnp.concatenate([kvbuf[slot, pg].astype(BF16) for pg in range(npc)], axis=0)
            acc = acc + _bdot(probs, values)
        return acc, lsum

    acc, lsum = lax.fori_loop(0, n_chunks // DECODE_UNROLL, v_chunks,
                              (jnp.zeros((GROUP_LANES, ATT_HD), F32), jnp.zeros((LANES, 1), F32)))
    top = lambda r: jnp.concatenate([r, jnp.zeros((LANES - r.shape[0], LANES), F32)], axis=0)
    et_new = top(new_pairs * e_new).T
    acc = acc + _bdot(et_new[0:GROUP_LANES, :], top(vn_ref[...]))
    lsum = lsum + jnp.sum(et_new, axis=1, keepdims=True)
    l16 = lsum[0:GROUP_LANES, :]
    for pg in range(1, npc):
        l16 = l16 + lsum[pg * GROUP_LANES:(pg + 1) * GROUP_LANES, :]
    o_ref[...] = (acc / l16).astype(BF16)


def _dsa_step(z2d, cache_k, cache_v, cache_kidx, page_table, *, d_model):
    off = _odd_columns(d_model)
    bsz = z2d.shape[0]
    n_pages = page_table.shape[1]
    n_phys, page, kvh, hd = cache_k.shape
    n_kv = kvh * hd
    heads = d_model // hd
    assert page == PAGE_SIZE and heads == GROUP_LANES and n_pages % PAGES_PER_CHUNK == 0
    n_chunks = n_pages // PAGES_PER_CHUNK
    topk = min(INDEX_TOPK, (n_pages * page + 1) // 4)
    q = z2d[:, off["q"]:off["q"] + d_model].reshape(bsz, kvh, heads // kvh, hd)
    qi = z2d[:, off["qi"]:off["qi"] + IDX_HEADS * IDX_DIM].reshape(bsz, IDX_HEADS, IDX_DIM)
    tile = lambda a: jnp.tile(a, (1, 1, PAGES_PER_CHUNK))
    qit = tile(jnp.transpose(qi, (0, 2, 1)))
    wq = tile(jnp.transpose(q.reshape(bsz, heads, hd), (0, 2, 1)))
    wl = tile(z2d[:, off["wi"]:off["wi"] + IDX_HEADS].reshape(bsz, 1, IDX_HEADS))
    kin = z2d[:, off["ki"]:off["ki"] + IDX_DIM].reshape(bsz, 1, IDX_DIM)
    new_rows = lambda c: jnp.pad(z2d[:, c:c + n_kv].reshape(bsz, kvh, hd), ((0, 0), (0, 8 - kvh), (0, 0)))
    kn, vn = new_rows(off["k"]), new_rows(off["v"])
    expand = jnp.repeat(jnp.eye(page, dtype=BF16), kvh, axis=0)
    per_seq = lambda a: pl.BlockSpec((None,) + a.shape[1:], lambda b, pt: (b, 0, 0))
    hbm = pl.BlockSpec(memory_space=pl.ANY)
    grid_spec = pltpu.PrefetchScalarGridSpec(
        num_scalar_prefetch=1,
        grid=(bsz,),
        in_specs=[per_seq(qit), per_seq(wq), per_seq(wl), per_seq(kin), per_seq(kn), per_seq(vn),
                  pl.BlockSpec(expand.shape, lambda b, pt: (0, 0)), hbm, hbm, hbm],
        out_specs=pl.BlockSpec((None, heads, hd), lambda b, pt: (b, 0, 0)),
        scratch_shapes=[
            pltpu.VMEM((IDX_RING, PAGES_PER_CHUNK, page, IDX_DIM), F32),
            pltpu.VMEM((KV_RING, PAGES_PER_CHUNK, page * kvh, hd), F32),
            pltpu.SemaphoreType.DMA((IDX_RING,)),
            pltpu.SemaphoreType.DMA((KV_RING,)),
            pltpu.VMEM((PAGES_PER_CHUNK * IDX_DIM, LANES), BF16),
            pltpu.VMEM((PAGES_PER_CHUNK * hd, LANES), BF16),
            pltpu.VMEM((n_chunks, page, LANES), jnp.int32),
            pltpu.VMEM((n_chunks, page * kvh, LANES), F32),
            pltpu.VMEM((8, LANES), jnp.int32),
        ])
    o = pl.pallas_call(
        functools.partial(_dsa_step_kernel, topk=topk),
        grid_spec=grid_spec,
        out_shape=jax.ShapeDtypeStruct((bsz, heads, hd), BF16),
        compiler_params=_params(("arbitrary",), 40),
        name="dsa_step",
    )(page_table, qit, wq, wl, kin, kn, vn, expand, cache_kidx, cache_k.reshape(n_phys, page * kvh, hd),
      cache_v.reshape(n_phys, page * kvh, hd))
    return o.reshape(bsz, d_model)


def _dsa_prompt(z2d, *, batch, seq, d_model, qb):
    off = _odd_columns(d_model)
    nq = seq // qb
    n_kv = ATT_KV_HEADS * ATT_HD
    topk = min(INDEX_TOPK, seq // 4)
    n_ext = math.gcd(nq, DSA_EXTENTS)
    nqc = nq // n_ext
    outs = []
    for c in range(n_ext):
        q0, nk = c * nqc, (c + 1) * nqc * qb
        rows = lambda w, col, q0=q0: pl.BlockSpec((qb, w), lambda b, i: (b * nq + q0 + i, col // w))
        whole = lambda w, col: pl.BlockSpec((seq, w), lambda b, i: (b, col // w))
        outs.append(pl.pallas_call(
            functools.partial(_dsa_kernel, topk=topk, nk=nk, q0=q0),
            grid=(batch, nqc),
            in_specs=[rows(d_model, off["q"]), rows(IDX_HEADS * IDX_DIM, off["qi"]), rows(LANES, off["wi"]),
                      whole(IDX_DIM, off["ki"]), whole(n_kv, off["k"]), whole(n_kv, off["v"])],
            out_specs=pl.BlockSpec((qb, d_model), lambda b, i: (b * nqc + i, 0)),
            out_shape=jax.ShapeDtypeStruct((batch * nqc * qb, d_model), BF16),
            scratch_shapes=[pltpu.VMEM((nk, IDX_DIM), BF16), pltpu.VMEM((nk, n_kv), BF16), pltpu.VMEM((nk, n_kv), BF16),
                            pltpu.VMEM((qb, nk), jnp.int32), pltpu.VMEM((qb, 1), jnp.int32)],
            compiler_params=_params(("parallel", "arbitrary"), 56),
            name="dsa_prompt",
        )(z2d, z2d, z2d, z2d, z2d, z2d).reshape(batch, nqc * qb, d_model))
    return jnp.concatenate(outs, axis=1).reshape(batch * seq, d_model)


def _trunk(x2d, mods, wts, cfg, even_fn, odd_fn):
    tm, tf, tpb = cfg["tm"], cfg["tf"], cfg["tpb"]
    ffn = functools.partial(_ffn, tm=tm, tf=tf, tiles_per_batch=tpb, vmem_mib=cfg["ffn_vmem"])
    x = x2d
    states = []
    depth = wts["norm_g"].shape[0]
    for layer in range(depth):
        g, mod = wts["norm_g"][layer], mods[layer]
        x = ffn(x, g[0], mod, 0, wts["w_ffn_in"], wts["w_ffn_out"], (layer, 0), wts["final_g"], final_norm=False)
        x, st = (even_fn if layer % 2 == 0 else odd_fn)(x, g[1], mod)
        states.append(st)
        x = ffn(x, g[2], mod, 2, wts["w_ffn_in"], wts["w_ffn_out"], (layer, 1), wts["final_g"],
                final_norm=layer == depth - 1)
    return x, states


def kernel(x_prompt, x_sample, c_prompt, c_sample, state_ret, state_s5_re, state_s5_im, cache_k, cache_v, cache_kidx,
           page_table, norm_g, w_ada, b_ada, w_ffn_in, w_ffn_out, w_in_even, w_out_even, s5_a_re, s5_a_im, s5_log_dt,
           s5_b_re, s5_b_im, s5_c_re, s5_c_im, s5_d, w_glu, b_glu, w_in_odd, w_out_odd, final_g):
    bp, lp, d = x_prompt.shape
    bs, ls, _ = x_sample.shape
    assert ls == 1
    depth = w_ada.shape[0]
    n_mod = N_SUBLAYERS * 3

    rows = -(-(bp + bs) // 8) * 8
    c_all = jnp.concatenate([c_prompt, c_sample, jnp.zeros((rows - bp - bs, d), F32)], axis=0)
    mod_all = _ada_mod(c_all, w_ada, b_ada)
    mods_p = [mod_all[l, :bp].reshape(bp, n_mod, 1, d) for l in range(depth)]
    mods_s = [jnp.transpose(mod_all[l, bp:bp + bs].reshape(bs, n_mod, d), (1, 0, 2))[None] for l in range(depth)]

    p = dict(w_in_even=w_in_even, w_out_even=w_out_even, s5_a_re=s5_a_re, s5_a_im=s5_a_im, s5_log_dt=s5_log_dt,
             s5_b_re=s5_b_re, s5_b_im=s5_b_im, s5_c_re=s5_c_re, s5_c_im=s5_c_im, s5_d=s5_d, w_glu=w_glu, b_glu=b_glu)
    sp = _s5_prepare(p)
    w_odd = _odd_weight(w_in_odd, d)
    off = _odd_columns(d)
    n_kv = ATT_KV_HEADS * ATT_HD
    wts = dict(norm_g=norm_g, w_ffn_in=w_ffn_in, w_ffn_out=w_ffn_out, final_g=final_g)
    odd_tn = 768

    def cache_parts(z, b, l):
        return (z[:, off["k"]:off["k"] + n_kv].reshape(b, l, ATT_KV_HEADS, ATT_HD),
                z[:, off["v"]:off["v"] + n_kv].reshape(b, l, ATT_KV_HEADS, ATT_HD),
                z[:, off["ki"]:off["ki"] + IDX_DIM].reshape(b, l, IDX_DIM))

    cfg_p = dict(tm=min(1024, lp), tf=256, ffn_vmem=60)
    cfg_p["tpb"] = lp // cfg_p["tm"]

    def even_p(x, g, mod):
        zero_s5 = jnp.zeros((bp,) + state_s5_re.shape[1:], F32)
        x, s_ret, sr, si = _even_layer(x, g, mod, p, jnp.zeros((bp,) + state_ret.shape[1:], F32), zero_s5, zero_s5,
                                       batch=bp, seq=lp, pos0=0, tm=cfg_p["tm"], tiles_per_batch=cfg_p["tpb"],
                                       rb=min(256, lp), chunk=min(RET_CHUNK, lp), s5_t=min(128, lp), sp=sp)
        return x, (s_ret, sr, si)

    def odd_p(x, g, mod):
        z = _proj(x, g, mod, 1, w_odd, tm=cfg_p["tm"], tn=odd_tn, tiles_per_batch=cfg_p["tpb"], vmem_mib=52)
        o = _dsa_prompt(z, batch=bp, seq=lp, d_model=d, qb=min(128, lp))
        x = _outproj([(o, d, None)], w_out_odd, x, mod, tm=cfg_p["tm"], tn=1024, tiles_per_batch=cfg_p["tpb"],
                     vmem_mib=48)
        return x, cache_parts(z, bp, lp)

    y_p, (ev_p, od_p) = _trunk(x_prompt.reshape(bp * lp, d), mods_p, wts, cfg_p, even_p, odd_p)

    past_len = page_table.shape[1] * cache_k.shape[1]
    cfg_s = dict(tm=bs, tf=512, tpb=1, ffn_vmem=40)

    def even_s(x, g, mod):
        x, s_ret, sr, si = _even_layer(x, g, mod, p, state_ret, state_s5_re, state_s5_im, batch=bs, seq=1,
                                       pos0=past_len, tm=bs, tiles_per_batch=1, rb=1, chunk=1, s5_t=1, sp=sp)
        return x, (s_ret, sr, si)

    def odd_s(x, g, mod):
        z = _proj(x, g, mod, 1, w_odd, tm=bs, tn=odd_tn, tiles_per_batch=1, vmem_mib=40)
        o = _dsa_step(z, cache_k, cache_v, cache_kidx, page_table, d_model=d)
        x = _outproj([(o, d, None)], w_out_odd, x, mod, tm=bs, tn=1024, tiles_per_batch=1, vmem_mib=40)
        return x, cache_parts(z, bs, 1)

    y_s, (ev_s, od_s) = _trunk(x_sample.reshape(bs, d), mods_s, wts, cfg_s, even_s, odd_s)

    return (y_p.reshape(bp, lp, d), y_s.reshape(bs, 1, d), *ev_p, *od_p, *ev_s, *od_s)
```

```python
import functools
import math

import jax
import jax.numpy as jnp
from jax import lax
from jax.experimental import pallas as pl
from jax.experimental.pallas import tpu as pltpu

F32 = jnp.float32
BF16 = jnp.bfloat16

EPS = 1e-6
FFN_RES = 0.5
N_SUBLAYERS = 3
RET_HEADS = 4
RET_CHUNK = 128
ROPE_BASE = 10000.0
S5_GROUP_CH = 16
S5_STATE = 64
ATT_HD = 128
ATT_KV_HEADS = 4
IDX_HEADS = 16
IDX_DIM = 128
INDEX_TOPK = 256
PAGE_SIZE = 128

LANES = 128
MIB = 1024 * 1024
NEG_INF = float("-inf")


def _params(sem, vmem_mib):
    return pltpu.CompilerParams(dimension_semantics=sem, vmem_limit_bytes=vmem_mib * MIB)


def _bdot(a, b):
    return jnp.dot(a.astype(BF16), b.astype(BF16), preferred_element_type=F32)


def _bdot_nt(a, b):
    return lax.dot_general(a.astype(BF16), b.astype(BF16), (((1,), (1,)), ((), ())),
                           preferred_element_type=F32)


def _bdot_tn(a, b):
    return lax.dot_general(a.astype(BF16), b.astype(BF16), (((0,), (0,)), ((), ())),
                           preferred_element_type=F32)


def _silu(x):
    return x * jax.nn.sigmoid(x)


def _rms(x, g):
    ms = jnp.mean(x * x, axis=-1, keepdims=True)
    return x * lax.rsqrt(ms + EPS) * g


def _row_chunks(n_rows, chunk, fn):
    if n_rows <= chunk:
        fn(slice(0, n_rows))
        return
    assert n_rows % chunk == 0

    def body(i, c):
        fn(pl.ds(pl.multiple_of(i * chunk, chunk), chunk))
        return c

    lax.fori_loop(0, n_rows // chunk, body, 0)


ROW_CHUNK = 256


def _norm_mod_to(h_ref, x_ref, g_ref, sc_ref, sh_ref):
    n_rows = x_ref.shape[0]
    assert sc_ref.shape[0] == 1 or n_rows <= ROW_CHUNK

    def fn(rows):
        y = _rms(x_ref[rows, :], g_ref[...])
        h_ref[rows, :] = (y * (1.0 + sc_ref[...]) + sh_ref[...]).astype(BF16)

    _row_chunks(n_rows, ROW_CHUNK, fn)


def _ada_kernel(c_ref, w_ref, b_ref, o_ref):
    o_ref[...] = _bdot(_silu(c_ref[...]), w_ref[...]) + b_ref[...]


def _ada_mod(c_all, w_ada, b_ada):
    depth, d, n = w_ada.shape
    rows = c_all.shape[0]
    tn = 1024
    return pl.pallas_call(
        _ada_kernel,
        grid=(depth, n // tn),
        in_specs=[
            pl.BlockSpec((rows, d), lambda l, j: (0, 0)),
            pl.BlockSpec((None, d, tn), lambda l, j: (l, 0, j)),
            pl.BlockSpec((None, 1, tn), lambda l, j: (l, 0, j)),
        ],
        out_specs=pl.BlockSpec((None, rows, tn), lambda l, j: (l, 0, j)),
        out_shape=jax.ShapeDtypeStruct((depth, rows, n), F32),
        compiler_params=_params(("arbitrary", "arbitrary"), 40),
        name="ada_mod",
    )(c_all, w_ada, b_ada.reshape(depth, 1, n))


def _mod_specs(mod, sub, tiles_per_batch, which):
    _, _, r, d = mod.shape
    specs = []
    for k in which:
        idx = sub * 3 + k
        specs.append(pl.BlockSpec((None, None, r, d),
                                  lambda i, j, idx=idx: (i // tiles_per_batch, idx, 0, 0)))
    return specs


def _ffn_kernel(x_ref, g_ref, sh_ref, sc_ref, gt_ref, wa_ref, wb_ref, wo_ref, fg_ref, o_ref, h_ref,
                *, final_norm):
    j = pl.program_id(1)
    nj = pl.num_programs(1)

    @pl.when(j == 0)
    def _():
        _norm_mod_to(h_ref, x_ref, g_ref, sc_ref, sh_ref)
        o_ref[...] = jnp.zeros(o_ref.shape, F32)

    h = h_ref[...]
    a = jnp.dot(h, wa_ref[...].astype(BF16), preferred_element_type=F32)
    b = jnp.dot(h, wb_ref[...].astype(BF16), preferred_element_type=F32)
    o_ref[...] += _bdot(_silu(a) * b, wo_ref[...])

    @pl.when(j == nj - 1)
    def _():
        def fn(rows):
            y = x_ref[rows, :] + (FFN_RES * gt_ref[...]) * o_ref[rows, :]
            if final_norm:
                y = _rms(y, fg_ref[...])
            o_ref[rows, :] = y

        _row_chunks(x_ref.shape[0], ROW_CHUNK, fn)


def _ffn(x2d, norm_g, mod, sub, w_in, w_out, wsel, final_g, *, tm, tf, tiles_per_batch, final_norm, vmem_mib):
    m, d = x2d.shape
    f = w_out.shape[2]
    nf = f // tf
    assert m % tm == 0 and f % tf == 0
    row = lambda i, j: (0, 0)
    l, k = wsel
    return pl.pallas_call(
        functools.partial(_ffn_kernel, final_norm=final_norm),
        grid=(m // tm, nf),
        in_specs=[
            pl.BlockSpec((tm, d), lambda i, j: (i, 0)),
            pl.BlockSpec((1, d), row),
            *_mod_specs(mod, sub, tiles_per_batch, (0, 1, 2)),
            pl.BlockSpec((None, None, d, tf), lambda i, j: (l, k, 0, j)),
            pl.BlockSpec((None, None, d, tf), lambda i, j: (l, k, 0, nf + j)),
            pl.BlockSpec((None, None, tf, d), lambda i, j: (l, k, j, 0)),
            pl.BlockSpec((1, d), row),
        ],
        out_specs=pl.BlockSpec((tm, d), lambda i, j: (i, 0)),
        out_shape=jax.ShapeDtypeStruct((m, d), F32),
        scratch_shapes=[pltpu.VMEM((tm, d), BF16)],
        compiler_params=_params(("parallel", "arbitrary"), vmem_mib),
        name="ffn",
    )(x2d, norm_g.reshape(1, d), mod, mod, mod, w_in, w_in, w_out, final_g.reshape(1, d))


def _proj_kernel(x_ref, g_ref, sh_ref, sc_ref, w_ref, o_ref, h_ref):
    @pl.when(pl.program_id(1) == 0)
    def _():
        _norm_mod_to(h_ref, x_ref, g_ref, sc_ref, sh_ref)

    o_ref[...] = jnp.dot(h_ref[...], w_ref[...].astype(BF16), preferred_element_type=F32)


def _proj(x2d, norm_g, mod, sub, w, *, tm, tn, tiles_per_batch, vmem_mib, single_x=False):
    m, d = x2d.shape
    n = w.shape[1]
    assert m % tm == 0 and n % tn == 0
    x_mode = dict(pipeline_mode=pl.Buffered(1)) if single_x else {}
    return pl.pallas_call(
        _proj_kernel,
        grid=(m // tm, n // tn),
        in_specs=[
            pl.BlockSpec((tm, d), lambda i, j: (i, 0), **x_mode),
            pl.BlockSpec((1, d), lambda i, j: (0, 0)),
            *_mod_specs(mod, sub, tiles_per_batch, (0, 1)),
            pl.BlockSpec((d, tn), lambda i, j: (0, j)),
        ],
        out_specs=pl.BlockSpec((tm, tn), lambda i, j: (i, j)),
        out_shape=jax.ShapeDtypeStruct((m, n), F32),
        scratch_shapes=[pltpu.VMEM((tm, d), BF16)],
        compiler_params=_params(("parallel", "arbitrary"), vmem_mib),
        name="mixer_in_proj",
    )(x2d, norm_g.reshape(1, d), mod, mod, w)


def _outproj_kernel(*refs, n_parts):
    a_refs = refs[:n_parts]
    w_ref, x_ref, gt_ref, o_ref = refs[n_parts:]
    acc = None
    k0 = 0
    for a_ref in a_refs:
        kk = a_ref.shape[1]
        p = jnp.dot(a_ref[...], w_ref[k0:k0 + kk, :].astype(BF16), preferred_element_type=F32)
        acc = p if acc is None else acc + p
        k0 += kk
    o_ref[...] = x_ref[...] + gt_ref[...] * acc


def _outproj(a_parts, w, x2d, mod, *, tm, tn, tiles_per_batch, vmem_mib):
    m, d = x2d.shape
    k = w.shape[0]
    assert sum(kw for _, kw, _ in a_parts) == k and m % tm == 0 and d % tn == 0
    _, _, r, _ = mod.shape
    return pl.pallas_call(
        functools.partial(_outproj_kernel, n_parts=len(a_parts)),
        grid=(m // tm, d // tn),
        in_specs=[
            *[pl.BlockSpec((tm, kw), imap or (lambda i, j: (i, 0))) for _, kw, imap in a_parts],
            pl.BlockSpec((k, tn), lambda i, j: (0, j)),
            pl.BlockSpec((tm, tn), lambda i, j: (i, j)),
            pl.BlockSpec((None, None, r, tn), lambda i, j: (i // tiles_per_batch, 5, 0, j)),
        ],
        out_specs=pl.BlockSpec((tm, tn), lambda i, j: (i, j)),
        out_shape=jax.ShapeDtypeStruct((m, d), F32),
        compiler_params=_params(("parallel", "arbitrary"), vmem_mib),
        name="mixer_out_proj",
    )(*[a for a, _, _ in a_parts], w, x2d, mod)


def _rot(x1, x2, cos, sin):
    return jnp.concatenate([x1 * cos - x2 * sin, x1 * sin + x2 * cos], axis=-1)


def _head_norm(o):
    mu = jnp.mean(o, axis=-1, keepdims=True)
    var = jnp.mean(jnp.square(o - mu), axis=-1, keepdims=True)
    return (o - mu) * lax.rsqrt(var + 1e-5)


def _ret_log_g(h):
    return math.log1p(-(2.0 ** (-5.0 - h)))


def _ret_kernel(q_ref, k_ref, v_ref, g_ref, cos_ref, sin_ref, s0_ref, o_ref, s_ref, *, chunk):
    c = pl.program_id(1)
    rb = q_ref.shape[0]
    dk = q_ref.shape[1] // RET_HEADS
    half = dk // 2

    @pl.when(c == 0)
    def _():
        s_ref[...] = s0_ref[...]

    ri = lax.broadcasted_iota(jnp.int32, (chunk, chunk), 0).astype(F32)
    ci = lax.broadcasted_iota(jnp.int32, (chunk, chunk), 1).astype(F32)
    diff = ri - ci
    rowf = lax.broadcasted_iota(jnp.int32, (chunk, dk), 0).astype(F32)
    for h in range(RET_HEADS):
        lg = _ret_log_g(h)
        dmat = jnp.where(diff >= 0, jnp.exp(jnp.maximum(diff, 0.0) * lg), 0.0)
        q_dec = jnp.exp((rowf + 1.0) * lg)
        k_dec = jnp.exp((chunk - 1.0 - rowf) * lg)
        c_dec = math.exp(chunk * lg)
        c0 = h * dk
        for ck in range(rb // chunk):
            rows = slice(ck * chunk, (ck + 1) * chunk)
            cos = cos_ref[rows, :]
            sin = sin_ref[rows, :]
            qr = _rot(q_ref[rows, c0:c0 + half], q_ref[rows, c0 + half:c0 + dk], cos, sin)
            kr = _rot(k_ref[rows, c0:c0 + half], k_ref[rows, c0 + half:c0 + dk], cos, sin) * (dk ** -0.5)
            vb = v_ref[rows, c0:c0 + dk].astype(BF16)
            qb = qr.astype(BF16)
            a = _bdot_nt(qb, kr) * dmat
            s = s_ref[h]
            o = _bdot(a, vb) + _bdot(qb, s) * q_dec
            s_ref[h] = s * c_dec + _bdot_tn(kr * k_dec, vb)
            o_ref[rows, c0:c0 + dk] = (_head_norm(o) * _silu(g_ref[rows, c0:c0 + dk])).astype(BF16)


def _retention(z2d, col0, cos, sin, state0, *, batch, seq, rb, chunk):
    _, h, dk, dv = state0.shape
    w = h * dk
    nb = seq // rb
    assert seq % rb == 0 and rb % chunk == 0 and col0 % w == 0
    cb = col0 // w
    zspec = lambda k: pl.BlockSpec((rb, w), lambda b, c, k=k: (b * nb + c, cb + k))
    tab = pl.BlockSpec((rb, dk // 2), lambda b, c: (c, 0))
    st = pl.BlockSpec((None, h, dk, dv), lambda b, c: (b, 0, 0, 0))
    return pl.pallas_call(
        functools.partial(_ret_kernel, chunk=chunk),
        grid=(batch, nb),
        in_specs=[zspec(0), zspec(1), zspec(2), zspec(3), tab, tab, st],
        out_specs=[pl.BlockSpec((rb, w), lambda b, c: (b * nb + c, 0)), st],
        out_shape=[jax.ShapeDtypeStruct((batch * seq, w), BF16),
                   jax.ShapeDtypeStruct(state0.shape, F32)],
        compiler_params=_params(("parallel", "arbitrary"), 40),
        name="retention",
    )(z2d, z2d, z2d, z2d, cos, sin, state0)


def _rope_tables(pos, half):
    freqs = ROPE_BASE ** (-jnp.arange(half, dtype=F32) / half)
    ang = pos.astype(F32)[:, None] * freqs[None, :]
    return jnp.cos(ang), jnp.sin(ang)


def _r16(x):
    return x.astype(BF16).astype(F32)


def _ret_step_kernel(q_ref, krow_ref, kcol_ref, v_ref, g_ref, cos_ref, sin_ref, cosc_ref, sinc_ref, s0_ref,
                     o_ref, s_ref):
    dk = q_ref.shape[1] // RET_HEADS
    half = dk // 2
    cos, sin = cos_ref[...], sin_ref[...]
    cosc, sinc = cosc_ref[...], sinc_ref[...]
    for h in range(RET_HEADS):
        dec = math.exp(_ret_log_g(h))
        c0 = h * dk
        qr = _r16(_rot(q_ref[:, c0:c0 + half], q_ref[:, c0 + half:c0 + dk], cos, sin))
        kr = _r16(_rot(krow_ref[:, c0:c0 + half], krow_ref[:, c0 + half:c0 + dk], cos, sin) * (dk ** -0.5))
        k1, k2 = kcol_ref[h, 0:half, :], kcol_ref[h, half:dk, :]
        kc = _r16(jnp.concatenate([k1 * cosc - k2 * sinc, k1 * sinc + k2 * cosc], axis=0) * (dk ** -0.5))
        v = _r16(v_ref[:, c0:c0 + dk])
        a = jnp.sum(qr * kr, axis=-1, keepdims=True)
        s = s0_ref[h]
        o = _r16(a) * v + _bdot(qr, s) * dec
        s_ref[h] = s * dec + kc * v
        o_ref[:, c0:c0 + dk] = (_head_norm(o) * _silu(g_ref[:, c0:c0 + dk])).astype(BF16)


def _retention_step(z2d, cos, sin, state0):
    b, h, dk, dv = state0.shape
    w = h * dk
    z3 = z2d.reshape(b, 1, z2d.shape[1])
    kcol = z2d[:, w:2 * w].reshape(b, h, dk, 1)
    half = dk // 2
    zspec = lambda k: pl.BlockSpec((None, 1, w), lambda i, k=k: (i, 0, k))
    row = pl.BlockSpec((1, half), lambda i: (0, 0))
    col = pl.BlockSpec((half, 1), lambda i: (0, 0))
    st = pl.BlockSpec((None, h, dk, dv), lambda i: (i, 0, 0, 0))
    og, s = pl.pallas_call(
        _ret_step_kernel,
        grid=(b,),
        in_specs=[zspec(0), zspec(1), pl.BlockSpec((None, h, dk, 1), lambda i: (i, 0, 0, 0)), zspec(2), zspec(3),
                  row, row, col, col, st],
        out_specs=[pl.BlockSpec((None, 1, w), lambda i: (i, 0, 0)), st],
        out_shape=[jax.ShapeDtypeStruct((b, 1, w), BF16), jax.ShapeDtypeStruct(state0.shape, F32)],
        compiler_params=_params(("parallel",), 32),
        name="retention_step",
    )(z3, z3, kcol, z3, z3, cos, sin, cos.reshape(half, 1), sin.reshape(half, 1), state0)
    return og.reshape(b, w), s


S5_BLOCKS = 8


def _s5_disc(ar, ai, dt):
    mag = jnp.exp(dt * ar)
    abr = mag * jnp.cos(dt * ai)
    abi = mag * jnp.sin(dt * ai)
    den = ar * ar + ai * ai
    xr = abr - 1.0
    return abr, abi, (xr * ar + abi * ai) / den, (abi * ar - xr * ai) / den


def _s5_param_kernel(ar_ref, ai_ref, ldt_ref, arr_ref, air_ref, br_ref, bi_ref, abr_ref, abi_ref, bbr_ref, bbi_ref):
    dt = jnp.exp(ldt_ref[...])
    abr_ref[...], abi_ref[...], _, _ = _s5_disc(ar_ref[...], ai_ref[...], dt)
    _, _, fr, fi = _s5_disc(arr_ref[...], air_ref[...], dt)
    br, bi = br_ref[...], bi_ref[...]
    bbr_ref[...] = fr * br - fi * bi
    bbi_ref[...] = fr * bi + fi * br


def _s5_prepare(p):
    g, st = p["s5_a_re"].shape
    ch = p["s5_d"].shape[1]
    rep = lambda a: jnp.repeat(a, ch, axis=1)
    abr, abi, bbr, bbi = pl.pallas_call(
        _s5_param_kernel,
        out_shape=[jax.ShapeDtypeStruct((g, st), F32)] * 2 + [jax.ShapeDtypeStruct((g, st * ch), F32)] * 2,
        name="s5_discretise",
    )(p["s5_a_re"], p["s5_a_im"], p["s5_log_dt"].reshape(g, 1), rep(p["s5_a_re"]), rep(p["s5_a_im"]),
      p["s5_b_re"].reshape(g, st * ch), p["s5_b_im"].reshape(g, st * ch))
    nb = S5_BLOCKS
    gl = g // nb
    eye = jnp.eye(gl, dtype=F32)
    w_in = lambda bb: jnp.einsum("bgpc,gh->bgchp", bb.reshape(nb, gl, st, ch), eye).reshape(nb, gl * ch, gl * st)
    w_out = lambda c: jnp.einsum("bgcp,gh->bgphc", c.reshape(nb, gl, ch, st), eye).reshape(nb, gl * st, gl * ch)
    return dict(
        abr=abr.reshape(1, g * st), abi=abi.reshape(1, g * st),
        wbr=w_in(bbr).astype(BF16), wbi=w_in(bbi).astype(BF16),
        wcr=w_out(p["s5_c_re"]).astype(BF16), wci=w_out(p["s5_c_im"]).astype(BF16),
        d=p["s5_d"].reshape(1, g * ch), w_glu=p["w_glu"], b_glu=p["b_glu"].reshape(1, -1))


S5_SCAN_LANES = 1024


def _s5_kernel(u_ref, x0r_ref, x0i_ref, abr_ref, abi_ref, wbr_ref, wbi_ref, wcr_ref, wci_ref, d_ref, wg_ref,
               bg_ref, y_ref, xr_ref, xi_ref, bur, bui, yb, wgb, *, ns, t):
    nblk = wbr_ref.shape[0]
    uw, sw = wbr_ref.shape[1], wbr_ref.shape[2]
    n_tiles = nblk * sw // LANES
    sub = 8

    @pl.when(pl.program_id(0) == 0)
    def _():
        xr_ref[...] = x0r_ref[...]
        xi_ref[...] = x0i_ref[...]
        wgb[...] = wg_ref[...].astype(BF16)

    tiles_per_blk = sw // LANES
    lane_tile = lambda j: slice(j * LANES, (j + 1) * LANES)
    for b in range(nblk):
        ub = u_ref[:, b * uw:(b + 1) * uw].astype(BF16)
        pr = jnp.dot(ub, wbr_ref[b], preferred_element_type=F32)
        pi = jnp.dot(ub, wbi_ref[b], preferred_element_type=F32)
        for k in range(tiles_per_blk):
            bur[b * tiles_per_blk + k] = pr[:, lane_tile(k)]
            bui[b * tiles_per_blk + k] = pi[:, lane_tile(k)]

    def affine(ar, ai, xr, xi, vr, vi):
        return ar * xr - ai * xi + vr, ar * xi + ai * xr + vi

    per_pass = S5_SCAN_LANES // LANES
    for j0 in range(0, n_tiles, per_pass):
        tiles = list(range(j0, min(j0 + per_pass, n_tiles)))
        if t == 1:
            for j in tiles:
                ar = jnp.broadcast_to(abr_ref[:, lane_tile(j)], (ns, LANES))
                ai = jnp.broadcast_to(abi_ref[:, lane_tile(j)], (ns, LANES))
                nr, ni = affine(ar, ai, xr_ref[:, lane_tile(j)], xi_ref[:, lane_tile(j)], bur[j], bui[j])
                bur[j], bui[j] = nr, ni
                xr_ref[:, lane_tile(j)], xi_ref[:, lane_tile(j)] = nr, ni
            continue
        assert 2 * ns == sub and t % 2 == 0
        ar = [jnp.broadcast_to(abr_ref[:, lane_tile(j)], (sub, LANES)) for j in tiles]
        ai = [jnp.broadcast_to(abi_ref[:, lane_tile(j)], (sub, LANES)) for j in tiles]
        top = lax.broadcasted_iota(jnp.int32, (sub, LANES), 0) < ns
        swap = lambda x: pltpu.roll(x, ns, axis=0)

        def step(i, carry, tiles=tiles, ar=ar, ai=ai):
            rows = pl.ds(pl.multiple_of(i * sub, sub), sub)
            out = []
            for k, j in enumerate(tiles):
                pr_, pi_ = carry[k]
                vr, vi = bur[j, rows, :], bui[j, rows, :]
                er, ei = affine(ar[k], ai[k], swap(pr_), swap(pi_), vr, vi)
                orr, oi = affine(ar[k], ai[k], swap(er), swap(ei), vr, vi)
                nr, ni = jnp.where(top, er, orr), jnp.where(top, ei, oi)
                bur[j, rows, :] = nr
                bui[j, rows, :] = ni
                out.append((nr, ni))
            return tuple(out)

        twice = lambda x: jnp.concatenate([x, x], axis=0)
        fin = lax.fori_loop(0, t // 2, step,
                            tuple((twice(xr_ref[:, lane_tile(j)]), twice(xi_ref[:, lane_tile(j)])) for j in tiles))
        for k, j in enumerate(tiles):
            xr_ref[:, lane_tile(j)] = fin[k][0][ns:, :]
            xi_ref[:, lane_tile(j)] = fin[k][1][ns:, :]

    blk_states = lambda ref, b: jnp.concatenate(
        [ref[b * tiles_per_blk + k] for k in range(tiles_per_blk)], axis=1).astype(BF16)
    for b in range(nblk):
        yb[:, b * uw:(b + 1) * uw] = (jnp.dot(blk_states(bur, b), wcr_ref[b], preferred_element_type=F32)
                                      - jnp.dot(blk_states(bui, b), wci_ref[b], preferred_element_type=F32))
    gl = jax.nn.gelu(yb[...] + d_ref[...] * u_ref[...])
    gate = jax.nn.sigmoid(jnp.dot(gl.astype(BF16), wgb[...], preferred_element_type=F32) + bg_ref[...])
    y_ref[...] = (gl * gate).astype(BF16)


def _s5(u2d, col_blk, sp, x0r, x0i, *, batch, t):
    w = sp["d"].shape[1]
    n_state = sp["abr"].shape[1]
    rows = batch * t
    assert u2d.shape[0] % rows == 0
    full = lambda a: pl.BlockSpec(a.shape, lambda c, nd=a.ndim: (0,) * nd)
    st_spec = pl.BlockSpec((batch, n_state), lambda c: (0, 0))
    consts = [sp["abr"], sp["abi"], sp["wbr"], sp["wbi"], sp["wcr"], sp["wci"], sp["d"], sp["w_glu"], sp["b_glu"]]
    return pl.pallas_call(
        functools.partial(_s5_kernel, ns=batch, t=t),
        grid=(u2d.shape[0] // rows,),
        in_specs=[pl.BlockSpec((rows, w), lambda c: (c, col_blk)), st_spec, st_spec] + [full(a) for a in consts],
        out_specs=[pl.BlockSpec((rows, w), lambda c: (c, 0)), st_spec, st_spec],
        out_shape=[jax.ShapeDtypeStruct((u2d.shape[0], w), BF16), jax.ShapeDtypeStruct((batch, n_state), F32),
                   jax.ShapeDtypeStruct((batch, n_state), F32)],
        scratch_shapes=[pltpu.VMEM((n_state // LANES, rows, LANES), F32), pltpu.VMEM((n_state // LANES, rows, LANES), F32),
                        pltpu.VMEM((rows, w), F32), pltpu.VMEM(sp["w_glu"].shape, BF16)],
        compiler_params=_params(("arbitrary",), 48),
        name="s5",
    )(u2d, x0r.reshape(batch, n_state), x0i.reshape(batch, n_state), *consts)


def _even_layer(x2d, norm_g, mod, p, ret0, s5r0, s5i0, *, batch, seq, pos0, tm, tiles_per_batch, rb, chunk, s5_t,
                sp=None):
    _, h, dk, _ = ret0.shape
    rw = h * dk
    if seq >= 2 * tm:
        z = _proj(x2d, norm_g, mod, 1, p["w_in_even"], tm=2 * tm, tn=512, tiles_per_batch=seq // (2 * tm), vmem_mib=52,
                  single_x=True)
    else:
        z = _proj(x2d, norm_g, mod, 1, p["w_in_even"], tm=tm, tn=1024, tiles_per_batch=tiles_per_batch, vmem_mib=52)
    cos, sin = _rope_tables(pos0 + jnp.arange(seq), dk // 2)
    if seq > 1:
        og, s_ret = _retention(z, 0, cos, sin, ret0, batch=batch, seq=seq, rb=rb, chunk=chunk)
    else:
        og, s_ret = _retention_step(z, cos, sin, ret0)
    sp = _s5_prepare(p) if sp is None else sp
    sw = sp["d"].shape[1]
    if seq > 1:
        u2d = jnp.transpose(z.reshape(batch, seq, -1)[:, :, 4 * rw:4 * rw + sw], (1, 0, 2)).reshape(seq * batch, sw)
        y, sr, si = _s5(u2d, 0, sp, s5r0, s5i0, batch=batch, t=s5_t)
        y_part = (y.reshape(seq, batch * sw), sw, lambda i, j: (i % tiles_per_batch, i // tiles_per_batch))
    else:
        y, sr, si = _s5(z, (4 * rw) // sw, sp, s5r0, s5i0, batch=batch, t=1)
        y_part = (y, sw, None)
    x_new = _outproj([(og, rw, None), y_part], p["w_out_even"], x2d, mod, tm=tm, tn=1024,
                     tiles_per_batch=tiles_per_batch, vmem_mib=48)
    return x_new, s_ret, sr.reshape(s5r0.shape), si.reshape(s5i0.shape)


INT32_MIN = -2 ** 31


def _sortable_keys(score):
    score = jnp.where(score == 0.0, 0.0, score)
    bits = lax.bitcast_convert_type(score, jnp.int32)
    return bits ^ (lax.shift_right_arithmetic(bits, 31) & 0x7FFFFFFF)


def _kth_largest(count_ge, shape, k):
    kf = jnp.float32(k)
    prefix = jnp.where(count_ge(jnp.zeros(shape, jnp.int32)) >= kf, 0, INT32_MIN).astype(jnp.int32)

    def body(i, prefix):
        cand = prefix | lax.shift_left(jnp.int32(1), 30 - i)
        return jnp.where(count_ge(cand) >= kf, cand, prefix)

    return lax.fori_loop(0, 31, body, prefix)


def _kth_largest_wide(count_ge, k, digit_bits=4):
    kf = jnp.float32(k)
    prefix = jnp.where(count_ge(jnp.zeros((1, 1), jnp.int32)) >= kf, 0, INT32_MIN).astype(jnp.int32)
    hi = 31
    while hi > 0:
        lo = max(hi - digit_bits, 0)
        digit = jnp.zeros((1, 1), jnp.int32)
        for d in range(1, 1 << (hi - lo)):
            digit = digit + jnp.where(count_ge(prefix | (d << lo)) >= kf, 1, 0)
        prefix = prefix | (digit * (1 << lo))
        hi = lo
    return prefix


def _odd_columns(d_model):
    n_q = d_model
    n_kv = ATT_KV_HEADS * ATT_HD
    n_qi = IDX_HEADS * IDX_DIM
    off = dict(q=0, qi=n_q, k=n_q + n_qi, v=n_q + n_qi + n_kv, ki=n_q + n_qi + 2 * n_kv)
    off["wi"] = off["ki"] + IDX_DIM
    off["end"] = off["wi"] + LANES
    return off


def _odd_weight(w_in_odd, d_model):
    n_kv = ATT_KV_HEADS * ATT_HD
    n_qi = IDX_HEADS * IDX_DIM
    cuts = [d_model, d_model + n_kv, d_model + 2 * n_kv, d_model + 2 * n_kv + n_qi, d_model + 2 * n_kv + n_qi + IDX_DIM]
    q, k, v, qi, ki, wi = jnp.split(w_in_odd, cuts, axis=1)
    pad = jnp.zeros((w_in_odd.shape[0], LANES - wi.shape[1]), w_in_odd.dtype)
    return jnp.concatenate([q, qi, k, v, ki, wi, pad], axis=1)


DSA_EXTENTS = 4


def _dsa_kernel(q_ref, qi_ref, wi_ref, ki_ref, k_ref, v_ref, o_ref, kib, kb, vb, key_ref, cut_ref, *, topk, nk, q0):
    i = pl.program_id(1)
    qb = q_ref.shape[0]
    group = q_ref.shape[1] // ATT_HD // ATT_KV_HEADS

    @pl.when(i == 0)
    def _():
        kib[...] = ki_ref[0:nk, :].astype(BF16)
        kb[...] = k_ref[0:nk, :].astype(BF16)
        vb[...] = v_ref[0:nk, :].astype(BF16)

    score = jnp.zeros((qb, nk), F32)
    for h in range(IDX_HEADS):
        s = _bdot_nt(qi_ref[:, h * IDX_DIM:(h + 1) * IDX_DIM], kib[...]) * (IDX_DIM ** -0.5)
        score = score + jnp.maximum(s, 0.0) * wi_ref[:, h:h + 1]
    score = score * (IDX_HEADS ** -0.5)
    q_pos = (q0 + i) * qb + lax.broadcasted_iota(jnp.int32, (qb, nk), 0)
    col = lax.broadcasted_iota(jnp.int32, (qb, nk), 1)
    key_ref[...] = _sortable_keys(jnp.where(col <= q_pos, score, NEG_INF))

    count = lambda hit: jnp.sum(hit, axis=-1, keepdims=True)
    one = lambda cond: jnp.where(cond, 1.0, 0.0)
    thr = _kth_largest(lambda t: count(one(key_ref[...] >= t)), (qb, 1), topk)

    cut_ref[...] = jnp.full((qb, 1), nk, jnp.int32)

    @pl.when(jnp.max(count(one(key_ref[...] >= thr))) > topk)
    def _():
        need = topk - count(one(key_ref[...] > thr))

        def bit(j, cut):
            cand = cut + lax.shift_left(jnp.int32(1), nk.bit_length() - 1 - j)
            below = count(jnp.where(key_ref[...] == thr, one(col < cand), 0.0))
            return jnp.where(below < need, cand, cut)

        cut_ref[...] = lax.fori_loop(0, nk.bit_length(), bit, jnp.zeros((qb, 1), jnp.int32))

    kt = key_ref[...]
    keep = jnp.where(kt > thr, 0.0, jnp.where(kt == thr, jnp.where(col <= cut_ref[...], 0.0, NEG_INF), NEG_INF))
    bias = jnp.concatenate([jnp.where(col <= q_pos, keep, NEG_INF)] * group, axis=0)
    for n in range(ATT_KV_HEADS):
        heads = range(n * group, (n + 1) * group)
        qs = jnp.concatenate([q_ref[:, h * ATT_HD:(h + 1) * ATT_HD] for h in heads], axis=0)
        s = _bdot_nt(qs, kb[:, n * ATT_HD:(n + 1) * ATT_HD]) * (ATT_HD ** -0.5) + bias
        e = jnp.exp(s - jnp.max(s, axis=-1, keepdims=True))
        o = _bdot(e, vb[:, n * ATT_HD:(n + 1) * ATT_HD]) / jnp.sum(e, axis=-1, keepdims=True)
        for g, h in enumerate(heads):
            o_ref[:, h * ATT_HD:(h + 1) * ATT_HD] = o[g * qb:(g + 1) * qb, :].astype(BF16)


PAGES_PER_CHUNK = 8
GROUP_LANES = LANES // PAGES_PER_CHUNK
IDX_RING = 4
KV_RING = 8
DECODE_UNROLL = 2


def _group_sum(x):
    sh = 1
    while sh < GROUP_LANES:
        x = x + pltpu.roll(x, sh, axis=1)
        sh *= 2
    return x


def _group_spread(x):
    sh = 1
    while sh < GROUP_LANES:
        x = jnp.maximum(x, pltpu.roll(x, LANES - sh, axis=1))
        sh *= 2
    return x


def _across_groups(x, op):
    sh = GROUP_LANES
    while sh < LANES:
        x = op(x, pltpu.roll(x, sh, axis=1))
        sh *= 2
    return x


def _dsa_step_kernel(pt_ref, qit_ref, wq_ref, wl_ref, kin_ref, kn_ref, vn_ref, ex_ref, ci_hbm, ck_hbm, cv_hbm, o_ref,
                     ibuf, kvbuf, isem, kvsem, qit_s, wq_s, key_ref, att_ref, cut_ref, *, topk):
    b = pl.program_id(0)
    nb = pl.num_programs(0)
    npc = PAGES_PER_CHUNK
    n_chunks = key_ref.shape[0]
    assert GROUP_LANES == IDX_HEADS and n_chunks % DECODE_UNROLL == 0 and min(ibuf.shape[0], kvbuf.shape[0]) > DECODE_UNROLL

    def page_copy(src, buf, sem, seq, chunk, slot, pg):
        return pltpu.make_async_copy(src.at[pt_ref[seq, chunk * npc + pg]], buf.at[slot, pg], sem.at[slot])

    def start(src, buf, sem, seq, chunk, slot):
        for pg in range(npc):
            page_copy(src, buf, sem, seq, chunk, slot, pg).start()

    def wait(src, buf, sem, slot):
        for pg in range(npc):
            page_copy(src, buf, sem, 0, 0, slot, pg).wait()

    ri, rk = ibuf.shape[0], kvbuf.shape[0]
    kv_jobs = 2 * n_chunks

    def idx_issue(job):
        @pl.when(job < nb * n_chunks)
        def _():
            start(ci_hbm, ibuf, isem, job // n_chunks, job % n_chunks, job % ri)

    def kv_issue(job):
        seq, jj, slot = job // kv_jobs, job % kv_jobs, job % rk

        @pl.when(jnp.logical_and(job < nb * kv_jobs, jj < n_chunks))
        def _():
            start(ck_hbm, kvbuf, kvsem, seq, jj, slot)

        @pl.when(jnp.logical_and(job < nb * kv_jobs, jj >= n_chunks))
        def _():
            start(cv_hbm, kvbuf, kvsem, seq, jj - n_chunks, slot)

    @pl.when(b == 0)
    def _():
        for job in range(ri - DECODE_UNROLL):
            idx_issue(jnp.int32(job))
        for job in range(rk - DECODE_UNROLL):
            kv_issue(jnp.int32(job))

    def stream_group(g, first_job, ring, issue, src, buf, sem):
        jobs = [first_job + g * DECODE_UNROLL + k for k in range(DECODE_UNROLL)]
        for job in jobs:
            issue(job + ring - DECODE_UNROLL)
        for job in jobs:
            wait(src, buf, sem, job % ring)
        return [(g * DECODE_UNROLL + k, job % ring) for k, job in enumerate(jobs)]

    lane = lax.broadcasted_iota(jnp.int32, (1, LANES), 1)
    lane_grp = lax.shift_right_logical(lane, int(math.log2(GROUP_LANES)))
    last_in_grp = (lane & (GROUP_LANES - 1)) == GROUP_LANES - 1
    for pg in range(npc):
        qit_s[pg * IDX_DIM:(pg + 1) * IDX_DIM, :] = jnp.where(lane_grp == pg, qit_ref[...], 0.0).astype(BF16)
        wq_s[pg * ATT_HD:(pg + 1) * ATT_HD, :] = jnp.where(lane_grp == pg, wq_ref[...], 0.0).astype(BF16)
    w_lane = wl_ref[...]
    kv_rows = kvbuf.shape[2]
    q_per_kv = GROUP_LANES // ATT_KV_HEADS
    kv_of_lane = (lane & (GROUP_LANES - 1)) // q_per_kv
    pairs = lambda n: (lax.broadcasted_iota(jnp.int32, (n, LANES), 0) & (ATT_KV_HEADS - 1)) == kv_of_lane
    first_rows = lambda n: lax.broadcasted_iota(jnp.int32, (n, LANES), 0) < ATT_KV_HEADS

    def idx_score(raw):
        s = jnp.maximum(raw * (IDX_DIM ** -0.5), 0.0) * w_lane
        return _group_sum(s) * (IDX_HEADS ** -0.5)

    def idx_chunks(g, carry):
        for c, slot in stream_group(g, b * n_chunks, ri, idx_issue, ci_hbm, ibuf, isem):
            pages = jnp.concatenate([ibuf[slot, pg].astype(BF16) for pg in range(npc)], axis=1)
            acc = jnp.dot(pages, qit_s[...], preferred_element_type=F32)
            key_ref[c] = _sortable_keys(jnp.where(last_in_grp, idx_score(acc), NEG_INF))
        return carry

    lax.fori_loop(0, n_chunks // DECODE_UNROLL, idx_chunks, 0)
    rows8 = lambda r: jnp.broadcast_to(r, (8, r.shape[1]))
    new_raw = _bdot(rows8(kin_ref[...]), qit_ref[...])[0:1, :]
    key_new = _sortable_keys(jnp.where(lane == GROUP_LANES - 1, idx_score(new_raw), NEG_INF))

    assert n_chunks <= GROUP_LANES
    dense = key_ref[0]
    for c in range(1, n_chunks):
        dense = jnp.maximum(dense, pltpu.roll(key_ref[c], LANES - c, axis=1))

    def count_ge(t):
        tot = jnp.sum(jnp.where(dense >= t, 1.0, 0.0), axis=0, keepdims=True) + jnp.where(key_new >= t, 1.0, 0.0)
        return jnp.sum(tot, axis=1, keepdims=True)

    thr = _kth_largest_wide(count_ge, topk)

    n_pos = n_chunks * npc * PAGE_SIZE
    row_id = lax.broadcasted_iota(jnp.int32, (PAGE_SIZE, LANES), 0)
    cut_ref[...] = jnp.full(cut_ref.shape, 2 ** n_pos.bit_length() - 1, jnp.int32)

    @pl.when(jnp.max(count_ge(thr)) > topk)
    def _():
        total = lambda x: jnp.sum(jnp.sum(x, axis=0, keepdims=True), axis=1, keepdims=True)
        need = topk - total(jnp.where(dense > thr, 1.0, 0.0)) - total(jnp.where(key_new > thr, 1.0, 0.0))
        chunk_of_lane = GROUP_LANES - 1 - (lane & (GROUP_LANES - 1))
        pos = (chunk_of_lane * npc + lane_grp) * PAGE_SIZE + row_id
        cut = jnp.zeros((1, 1), jnp.int32)
        for bit in range(n_pos.bit_length() - 1, -1, -1):
            cand = cut + (1 << bit)
            below = total(jnp.where(dense == thr, jnp.where(pos < cand, 1.0, 0.0), 0.0))
            cut = jnp.where(below < need, cand, cut)
        cut_ref[...] = jnp.broadcast_to(cut, cut_ref.shape)

    cut = cut_ref[0:1, 0:1]
    kept = lambda keys, positions: jnp.where(keys > thr, 1.0, jnp.where(keys == thr,
                                                                       jnp.where(positions <= cut, 1.0, 0.0), 0.0))
    new_sel = jnp.max(kept(key_new, n_pos), axis=1, keepdims=True) > 0.0

    def k_chunks(g, carry):
        for c, slot in stream_group(g, b * kv_jobs, rk, kv_issue, ck_hbm, kvbuf, kvsem):
            pages = jnp.concatenate([kvbuf[slot, pg].astype(BF16) for pg in range(npc)], axis=1)
            acc = jnp.dot(pages, wq_s[...], preferred_element_type=F32)
            sel = _group_spread(kept(key_ref[c], (c * npc + lane_grp) * PAGE_SIZE + row_id))
            sel_rows = jnp.dot(ex_ref[...], sel.astype(BF16), preferred_element_type=F32)
            att_ref[c] = jnp.where(pairs(kv_rows), jnp.where(sel_rows > 0.0, acc * (ATT_HD ** -0.5), NEG_INF),
                                   NEG_INF)
        return carry

    lax.fori_loop(0, n_chunks // DECODE_UNROLL, k_chunks, 0)
    new_pairs = jnp.where(first_rows(8), jnp.where(pairs(8), 1.0, 0.0), 0.0)
    s_new = jnp.sum(new_pairs * _bdot(kn_ref[...], wq_ref[...]), axis=0, keepdims=True)
    s_new = jnp.where(new_sel, s_new * (ATT_HD ** -0.5), NEG_INF)
    m = jnp.max(jnp.max(att_ref[...], axis=0), axis=0, keepdims=True)
    m = jnp.maximum(_across_groups(m, jnp.maximum), s_new)
    e_new = jnp.where(lane < GROUP_LANES, jnp.exp(s_new - m), 0.0)

    def v_chunks(g, carry):
        acc, lsum = carry
        for c, slot in stream_group(g, b * kv_jobs + n_chunks, rk, kv_issue, cv_hbm, kvbuf, kvsem):
            et = jnp.exp(att_ref[c] - m).T
            lsum = lsum + jnp.sum(et, axis=1, keepdims=True)
            probs = jnp.concatenate([et[pg * GROUP_LANES:(pg + 1) * GROUP_LANES, :] for pg in range(npc)], axis=1)
            values = jnp.concatenate([kvbuf[slot, pg].astype(BF16) for pg in range(npc)], axis=0)
            acc = acc + _bdot(probs, values)
        return acc, lsum

    acc, lsum = lax.fori_loop(0, n_chunks // DECODE_UNROLL, v_chunks,
                              (jnp.zeros((GROUP_LANES, ATT_HD), F32), jnp.zeros((LANES, 1), F32)))
    top = lambda r: jnp.concatenate([r, jnp.zeros((LANES - r.shape[0], LANES), F32)], axis=0)
    et_new = top(new_pairs * e_new).T
    acc = acc + _bdot(et_new[0:GROUP_LANES, :], top(vn_ref[...]))
    lsum = lsum + jnp.sum(et_new, axis=1, keepdims=True)
    l16 = lsum[0:GROUP_LANES, :]
    for pg in range(1, npc):
        l16 = l16 + lsum[pg * GROUP_LANES:(pg + 1) * GROUP_LANES, :]
    o_ref[...] = (acc / l16).astype(BF16)


def _dsa_step(z2d, cache_k, cache_v, cache_kidx, page_table, *, d_model):
    off = _odd_columns(d_model)
    bsz = z2d.shape[0]
    n_pages = page_table.shape[1]
    n_phys, page, kvh, hd = cache_k.shape
    n_kv = kvh * hd
    heads = d_model // hd
    assert page == PAGE_SIZE and heads == GROUP_LANES and n_pages % PAGES_PER_CHUNK == 0
    n_chunks = n_pages // PAGES_PER_CHUNK
    topk = min(INDEX_TOPK, (n_pages * page + 1) // 4)
    q = z2d[:, off["q"]:off["q"] + d_model].reshape(bsz, kvh, heads // kvh, hd)
    qi = z2d[:, off["qi"]:off["qi"] + IDX_HEADS * IDX_DIM].reshape(bsz, IDX_HEADS, IDX_DIM)
    tile = lambda a: jnp.tile(a, (1, 1, PAGES_PER_CHUNK))
    qit = tile(jnp.transpose(qi, (0, 2, 1)))
    wq = tile(jnp.transpose(q.reshape(bsz, heads, hd), (0, 2, 1)))
    wl = tile(z2d[:, off["wi"]:off["wi"] + IDX_HEADS].reshape(bsz, 1, IDX_HEADS))
    kin = z2d[:, off["ki"]:off["ki"] + IDX_DIM].reshape(bsz, 1, IDX_DIM)
    new_rows = lambda c: jnp.pad(z2d[:, c:c + n_kv].reshape(bsz, kvh, hd), ((0, 0), (0, 8 - kvh), (0, 0)))
    kn, vn = new_rows(off["k"]), new_rows(off["v"])
    expand = jnp.repeat(jnp.eye(page, dtype=BF16), kvh, axis=0)
    per_seq = lambda a: pl.BlockSpec((None,) + a.shape[1:], lambda b, pt: (b, 0, 0))
    hbm = pl.BlockSpec(memory_space=pl.ANY)
    grid_spec = pltpu.PrefetchScalarGridSpec(
        num_scalar_prefetch=1,
        grid=(bsz,),
        in_specs=[per_seq(qit), per_seq(wq), per_seq(wl), per_seq(kin), per_seq(kn), per_seq(vn),
                  pl.BlockSpec(expand.shape, lambda b, pt: (0, 0)), hbm, hbm, hbm],
        out_specs=pl.BlockSpec((None, heads, hd), lambda b, pt: (b, 0, 0)),
        scratch_shapes=[
            pltpu.VMEM((IDX_RING, PAGES_PER_CHUNK, page, IDX_DIM), F32),
            pltpu.VMEM((KV_RING, PAGES_PER_CHUNK, page * kvh, hd), F32),
            pltpu.SemaphoreType.DMA((IDX_RING,)),
            pltpu.SemaphoreType.DMA((KV_RING,)),
            pltpu.VMEM((PAGES_PER_CHUNK * IDX_DIM, LANES), BF16),
            pltpu.VMEM((PAGES_PER_CHUNK * hd, LANES), BF16),
            pltpu.VMEM((n_chunks, page, LANES), jnp.int32),
            pltpu.VMEM((n_chunks, page * kvh, LANES), F32),
            pltpu.VMEM((8, LANES), jnp.int32),
        ])
    o = pl.pallas_call(
        functools.partial(_dsa_step_kernel, topk=topk),
        grid_spec=grid_spec,
        out_shape=jax.ShapeDtypeStruct((bsz, heads, hd), BF16),
        compiler_params=_params(("arbitrary",), 40),
        name="dsa_step",
    )(page_table, qit, wq, wl, kin, kn, vn, expand, cache_kidx, cache_k.reshape(n_phys, page * kvh, hd),
      cache_v.reshape(n_phys, page * kvh, hd))
    return o.reshape(bsz, d_model)


def _dsa_prompt(z2d, *, batch, seq, d_model, qb):
    off = _odd_columns(d_model)
    nq = seq // qb
    n_kv = ATT_KV_HEADS * ATT_HD
    topk = min(INDEX_TOPK, seq // 4)
    n_ext = math.gcd(nq, DSA_EXTENTS)
    nqc = nq // n_ext
    outs = []
    for c in range(n_ext):
        q0, nk = c * nqc, (c + 1) * nqc * qb
        rows = lambda w, col, q0=q0: pl.BlockSpec((qb, w), lambda b, i: (b * nq + q0 + i, col // w))
        whole = lambda w, col: pl.BlockSpec((seq, w), lambda b, i: (b, col // w))
        outs.append(pl.pallas_call(
            functools.partial(_dsa_kernel, topk=topk, nk=nk, q0=q0),
            grid=(batch, nqc),
            in_specs=[rows(d_model, off["q"]), rows(IDX_HEADS * IDX_DIM, off["qi"]), rows(LANES, off["wi"]),
                      whole(IDX_DIM, off["ki"]), whole(n_kv, off["k"]), whole(n_kv, off["v"])],
            out_specs=pl.BlockSpec((qb, d_model), lambda b, i: (b * nqc + i, 0)),
            out_shape=jax.ShapeDtypeStruct((batch * nqc * qb, d_model), BF16),
            scratch_shapes=[pltpu.VMEM((nk, IDX_DIM), BF16), pltpu.VMEM((nk, n_kv), BF16), pltpu.VMEM((nk, n_kv), BF16),
                            pltpu.VMEM((qb, nk), jnp.int32), pltpu.VMEM((qb, 1), jnp.int32)],
            compiler_params=_params(("parallel", "arbitrary"), 56),
            name="dsa_prompt",
        )(z2d, z2d, z2d, z2d, z2d, z2d).reshape(batch, nqc * qb, d_model))
    return jnp.concatenate(outs, axis=1).reshape(batch * seq, d_model)


def _trunk(x2d, mods, wts, cfg, even_fn, odd_fn):
    tm, tf, tpb = cfg["tm"], cfg["tf"], cfg["tpb"]
    ffn = functools.partial(_ffn, tm=tm, tf=tf, tiles_per_batch=tpb, vmem_mib=cfg["ffn_vmem"])
    x = x2d
    states = []
    depth = wts["norm_g"].shape[0]
    for layer in range(depth):
        g, mod = wts["norm_g"][layer], mods[layer]
        x = ffn(x, g[0], mod, 0, wts["w_ffn_in"], wts["w_ffn_out"], (layer, 0), wts["final_g"], final_norm=False)
        x, st = (even_fn if layer % 2 == 0 else odd_fn)(x, g[1], mod)
        states.append(st)
        x = ffn(x, g[2], mod, 2, wts["w_ffn_in"], wts["w_ffn_out"], (layer, 1), wts["final_g"],
                final_norm=layer == depth - 1)
    return x, states


def kernel(x_prompt, x_sample, c_prompt, c_sample, state_ret, state_s5_re, state_s5_im, cache_k, cache_v, cache_kidx,
           page_table, norm_g, w_ada, b_ada, w_ffn_in, w_ffn_out, w_in_even, w_out_even, s5_a_re, s5_a_im, s5_log_dt,
           s5_b_re, s5_b_im, s5_c_re, s5_c_im, s5_d, w_glu, b_glu, w_in_odd, w_out_odd, final_g):
    bp, lp, d = x_prompt.shape
    bs, ls, _ = x_sample.shape
    assert ls == 1
    depth = w_ada.shape[0]
    n_mod = N_SUBLAYERS * 3

    rows = -(-(bp + bs) // 8) * 8
    c_all = jnp.concatenate([c_prompt, c_sample, jnp.zeros((rows - bp - bs, d), F32)], axis=0)
    mod_all = _ada_mod(c_all, w_ada, b_ada)
    mods_p = [mod_all[l, :bp].reshape(bp, n_mod, 1, d) for l in range(depth)]
    mods_s = [jnp.transpose(mod_all[l, bp:bp + bs].reshape(bs, n_mod, d), (1, 0, 2))[None] for l in range(depth)]

    p = dict(w_in_even=w_in_even, w_out_even=w_out_even, s5_a_re=s5_a_re, s5_a_im=s5_a_im, s5_log_dt=s5_log_dt,
             s5_b_re=s5_b_re, s5_b_im=s5_b_im, s5_c_re=s5_c_re, s5_c_im=s5_c_im, s5_d=s5_d, w_glu=w_glu, b_glu=b_glu)
    sp = _s5_prepare(p)
    w_odd = _odd_weight(w_in_odd, d)
    off = _odd_columns(d)
    n_kv = ATT_KV_HEADS * ATT_HD
    wts = dict(norm_g=norm_g, w_ffn_in=w_ffn_in, w_ffn_out=w_ffn_out, final_g=final_g)
    odd_tn = 768

    def cache_parts(z, b, l):
        return (z[:, off["k"]:off["k"] + n_kv].reshape(b, l, ATT_KV_HEADS, ATT_HD),
                z[:, off["v"]:off["v"] + n_kv].reshape(b, l, ATT_KV_HEADS, ATT_HD),
                z[:, off["ki"]:off["ki"] + IDX_DIM].reshape(b, l, IDX_DIM))

    cfg_p = dict(tm=min(1024, lp), tf=256, ffn_vmem=60)
    cfg_p["tpb"] = lp // cfg_p["tm"]

    def even_p(x, g, mod):
        zero_s5 = jnp.zeros((bp,) + state_s5_re.shape[1:], F32)
        x, s_ret, sr, si = _even_layer(x, g, mod, p, jnp.zeros((bp,) + state_ret.shape[1:], F32), zero_s5, zero_s5,
                                       batch=bp, seq=lp, pos0=0, tm=cfg_p["tm"], tiles_per_batch=cfg_p["tpb"],
                                       rb=min(256, lp), chunk=min(RET_CHUNK, lp), s5_t=min(128, lp), sp=sp)
        return x, (s_ret, sr, si)

    def odd_p(x, g, mod):
        z = _proj(x, g, mod, 1, w_odd, tm=cfg_p["tm"], tn=odd_tn, tiles_per_batch=cfg_p["tpb"], vmem_mib=52)
        o = _dsa_prompt(z, batch=bp, seq=lp, d_model=d, qb=min(128, lp))
        x = _outproj([(o, d, None)], w_out_odd, x, mod, tm=cfg_p["tm"], tn=1024, tiles_per_batch=cfg_p["tpb"],
                     vmem_mib=48)
        return x, cache_parts(z, bp, lp)

    y_p, (ev_p, od_p) = _trunk(x_prompt.reshape(bp * lp, d), mods_p, wts, cfg_p, even_p, odd_p)

    past_len = page_table.shape[1] * cache_k.shape[1]
    cfg_s = dict(tm=bs, tf=512, tpb=1, ffn_vmem=40)

    def even_s(x, g, mod):
        x, s_ret, sr, si = _even_layer(x, g, mod, p, state_ret, state_s5_re, state_s5_im, batch=bs, seq=1,
                                       pos0=past_len, tm=bs, tiles_per_batch=1, rb=1, chunk=1, s5_t=1, sp=sp)
        return x, (s_ret, sr, si)

    def odd_s(x, g, mod):
        z = _proj(x, g, mod, 1, w_odd, tm=bs, tn=odd_tn, tiles_per_batch=1, vmem_mib=40)
        o = _dsa_step(z, cache_k, cache_v, cache_kidx, page_table, d_model=d)
        x = _outproj([(o, d, None)], w_out_odd, x, mod, tm=bs, tn=1024, tiles_per_batch=1, vmem_mib=40)
        return x, cache_parts(z, bs, 1)

    y_s, (ev_s, od_s) = _trunk(x_sample.reshape(bs, d), mods_s, wts, cfg_s, even_s, odd_s)

    return (y_p.reshape(bp, lp, d), y_s.reshape(bs, 1, d), *ev_p, *od_p, *ev_s, *od_s)
```
